```python
import math
import jax, jax.numpy as jnp
from jax import lax
import numpy as np

D_MODEL = 2048
BATCH = 1
SEQ = 8192
DEPTH = 2

GRID_W = 64
HEAD_DIM = 128
ROPE_THETA = 500000.0
ROPE_DIM = HEAD_DIM // 4
NORM_EPS = 1e-6

NA_HEADS = 8
NA_KH_MAX = 8
NA_KW = 16

SW_Q_HEADS = 8
SW_KV_HEADS = 2
SW_WINDOW = 128
SW_BLOCK = 128

DIFF_HEADS = 8
Q_BLOCK = 128

MEM_LEN = 256
MEM_HEADS = 4

D_FF = 4 * D_MODEL

NA_W = NA_HEADS * HEAD_DIM
SW_QW = SW_Q_HEADS * HEAD_DIM
SW_KVW = SW_KV_HEADS * HEAD_DIM
EVEN_IN = 3 * NA_W + SW_QW + 2 * SW_KVW
EVEN_OUT = NA_W + SW_QW
DIFF_W = DIFF_HEADS * 2 * HEAD_DIM
ODD_IN = 3 * DIFF_W
MEM_W = MEM_HEADS * HEAD_DIM
N_EVEN = (DEPTH + 1) // 2
N_ODD = DEPTH // 2

kernel_name = "hybrid_natten_swa_diffattn_encoder"


def rms_norm(x, g):
    xf = x.astype(jnp.float32)
    y = xf * lax.rsqrt(jnp.mean(xf * xf, axis=-1, keepdims=True) + NORM_EPS)
    return (y * g.astype(jnp.float32)).astype(x.dtype)


def rope_tables(seq):
    inv = 1.0 / (ROPE_THETA ** (jnp.arange(0, ROPE_DIM, 2, dtype=jnp.float32) / ROPE_DIM))
    ang = jnp.arange(seq, dtype=jnp.float32)[:, None] * inv[None, :]
    return jnp.cos(ang), jnp.sin(ang)


def partial_rope(x, cos, sin):
    half = ROPE_DIM // 2
    x1 = x[..., :half].astype(jnp.float32)
    x2 = x[..., half:ROPE_DIM].astype(jnp.float32)
    c = cos[None, :, None, :]
    s = sin[None, :, None, :]
    r1 = (x1 * c - x2 * s).astype(x.dtype)
    r2 = (x2 * c + x1 * s).astype(x.dtype)
    return jnp.concatenate([r1, r2, x[..., ROPE_DIM:]], axis=-1)


def neighbourhood_attention(q, k, v, rpb):
    B, S, H, dh = q.shape
    rows = S // GRID_W
    kh = min(NA_KH_MAX, rows)
    kw = NA_KW
    qg = q.reshape(B, rows, GRID_W, H, dh)
    kg = k.reshape(B, rows, GRID_W, H, dh)
    vg = v.reshape(B, rows, GRID_W, H, dh)
    col = jnp.arange(GRID_W)
    col_start = jnp.clip(col - kw // 2, 0, GRID_W - kw)
    col_idx = col_start[:, None] + jnp.arange(kw)[None, :]
    col_off = col_idx - col[:, None] + (NA_KW - 1)
    scale = dh ** -0.5

    def one_row(i):
        rs = jnp.clip(i - kh // 2, 0, rows - kh)
        k_band = lax.dynamic_slice_in_dim(kg, rs, kh, axis=1)
        v_band = lax.dynamic_slice_in_dim(vg, rs, kh, axis=1)
        k_win = jnp.take(k_band, col_idx, axis=2)
        v_win = jnp.take(v_band, col_idx, axis=2)
        q_row = lax.dynamic_index_in_dim(qg, i, axis=1, keepdims=False)
        s = jnp.einsum('bjhd,brjchd->bhjrc', q_row, k_win).astype(jnp.float32) * scale
        row_off = rs + jnp.arange(kh) - i + (NA_KH_MAX - 1)
        bias = rpb[:, row_off[:, None, None], col_off[None, :, :]]
        s = s + jnp.transpose(bias, (0, 2, 1, 3))[None].astype(jnp.float32)
        p = jax.nn.softmax(s.reshape(B, H, GRID_W, kh * kw), axis=-1)
        p = p.reshape(B, H, GRID_W, kh, kw).astype(v.dtype)
        return jnp.einsum('bhjrc,brjchd->bjhd', p, v_win)

    out = lax.map(one_row, jnp.arange(rows))
    return jnp.transpose(out, (1, 0, 2, 3, 4)).reshape(B, S, H, dh)


def sliding_window_gqa(q, k, v, sinks, cos, sin):
    B, S, Hq, dh = q.shape
    Hkv = k.shape[2]
    G = Hq // Hkv
    nb = S // SW_BLOCK
    q = partial_rope(q, cos, sin)
    k = partial_rope(k, cos, sin)
    pad = ((0, 0), (SW_BLOCK, SW_BLOCK), (0, 0), (0, 0))
    kp = jnp.pad(k, pad).reshape(B, nb + 2, SW_BLOCK, Hkv, dh)
    vp = jnp.pad(v, pad).reshape(B, nb + 2, SW_BLOCK, Hkv, dh)
    k_band = jnp.concatenate([kp[:, :-2], kp[:, 1:-1], kp[:, 2:]], axis=2)
    v_band = jnp.concatenate([vp[:, :-2], vp[:, 1:-1], vp[:, 2:]], axis=2)
    qb = q.reshape(B, nb, SW_BLOCK, Hkv, G, dh)
    s = jnp.einsum('bnqkgd,bnckd->bnkgqc', qb, k_band).astype(jnp.float32) * (dh ** -0.5)
    blk = jnp.arange(nb)[:, None] * SW_BLOCK
    qpos = blk + jnp.arange(SW_BLOCK)[None, :]
    kpos = blk - SW_BLOCK + jnp.arange(3 * SW_BLOCK)[None, :]
    kp_b = kpos[:, None, :]
    valid = (jnp.abs(qpos[:, :, None] - kp_b) <= SW_WINDOW) & (kp_b >= 0) & (kp_b < S)
    s = jnp.where(valid[None, :, None, None, :, :], s, -jnp.inf)
    sink = sinks.astype(jnp.float32).reshape(Hkv, G)[None, None, :, :, None, None]
    m = jnp.maximum(jnp.max(s, axis=-1, keepdims=True), sink)
    p = jnp.exp(s - m)
    denom = jnp.sum(p, axis=-1, keepdims=True) + jnp.exp(sink - m)
    p = (p / denom).astype(v.dtype)
    o = jnp.einsum('bnkgqc,bnckd->bnqkgd', p, v_band)
    return o.reshape(B, S, Hq, dh)


def differential_attention(q, k, v, lam_q1, lam_k1, lam_q2, lam_k2, subln_g, lambda_init, cos, sin):
    B, S, H, _, dh = q.shape
    q = partial_rope(q.reshape(B, S, H * 2, dh), cos, sin).reshape(B, S, H, 2, dh)
    k = partial_rope(k.reshape(B, S, H * 2, dh), cos, sin).reshape(B, S, H, 2, dh)
    f32 = jnp.float32
    lam = (jnp.exp(jnp.sum(lam_q1.astype(f32) * lam_k1.astype(f32)))
           - jnp.exp(jnp.sum(lam_q2.astype(f32) * lam_k2.astype(f32))) + lambda_init)
    nb = S // Q_BLOCK
    qb = jnp.transpose(q.reshape(B, nb, Q_BLOCK, H, 2, dh), (1, 0, 2, 3, 4, 5))
    scale = dh ** -0.5

    def one_block(qblk):
        s = jnp.einsum('bqhtd,bkhtd->bhtqk', qblk, k).astype(f32) * scale
        p = jax.nn.softmax(s, axis=-1)
        a = p[:, :, 0] - lam * p[:, :, 1]
        return jnp.einsum('bhqk,bkhe->bqhe', a.astype(v.dtype), v)

    o = lax.map(one_block, qb)
    o = jnp.transpose(o, (1, 0, 2, 3, 4)).reshape(B, S, H, 2 * dh)
    o = rms_norm(o, subln_g) * (1.0 - lambda_init)
    return o.reshape(B, S, H * 2 * dh)


def memory_cross_attention(h, mem_n, wq, wk, wv, wo):
    B, S, _ = h.shape
    M = mem_n.shape[1]
    q = (h @ wq).reshape(B, S, MEM_HEADS, HEAD_DIM)
    k = (mem_n @ wk).reshape(B, M, MEM_HEADS, HEAD_DIM)
    v = (mem_n @ wv).reshape(B, M, MEM_HEADS, HEAD_DIM)
    s = jnp.einsum('bshd,bmhd->bhsm', q, k).astype(jnp.float32) * (HEAD_DIM ** -0.5)
    p = jax.nn.softmax(s, axis=-1).astype(v.dtype)
    o = jnp.einsum('bhsm,bmhd->bshd', p, v).reshape(B, S, MEM_W)
    return o @ wo


def squared_relu_mlp(h, w_up, w_down):
    u = jax.nn.relu(h @ w_up)
    return (u * u) @ w_down


def setup_inputs(seed: int = 0) -> dict:
    key = jax.random.key(seed)
    ks = jax.random.split(key, 32)
    f32 = jnp.float32

    def w(k, shape, fan_in):
        return jax.random.normal(k, shape, f32) * fan_in ** -0.5

    def gain(k, shape):
        return 1.0 + 0.05 * jax.random.normal(k, shape, f32)

    return {
        "x": jax.random.normal(ks[0], (BATCH, SEQ, D_MODEL), f32),
        "mem": jax.random.normal(ks[1], (BATCH, MEM_LEN, D_MODEL), f32),
        "even_w_in": w(ks[2], (N_EVEN, D_MODEL, EVEN_IN), D_MODEL),
        "even_w_out": w(ks[3], (N_EVEN, EVEN_OUT, D_MODEL), EVEN_OUT),
        "na_rpb": 0.5 * jax.random.normal(ks[4], (N_EVEN, NA_HEADS, 2 * NA_KH_MAX - 1, 2 * NA_KW - 1), f32),
        "sw_sinks": jax.random.normal(ks[5], (N_EVEN, SW_Q_HEADS), f32),
        "odd_w_in": w(ks[6], (N_ODD, D_MODEL, ODD_IN), D_MODEL),
        "odd_w_out": w(ks[7], (N_ODD, DIFF_W, D_MODEL), DIFF_W),
        "diff_lam_q1": 0.1 * jax.random.normal(ks[8], (N_ODD, HEAD_DIM), f32),
        "diff_lam_k1": 0.1 * jax.random.normal(ks[9], (N_ODD, HEAD_DIM), f32),
        "diff_lam_q2": 0.1 * jax.random.normal(ks[10], (N_ODD, HEAD_DIM), f32),
        "diff_lam_k2": 0.1 * jax.random.normal(ks[11], (N_ODD, HEAD_DIM), f32),
        "diff_subln_g": gain(ks[12], (N_ODD, 2 * HEAD_DIM)),
        "mix_pre_g": gain(ks[13], (DEPTH, D_MODEL)),
        "mix_post_g": gain(ks[14], (DEPTH, D_MODEL)),
        "mem_norm_g": gain(ks[15], (DEPTH, D_MODEL)),
        "mem_pre_g": gain(ks[16], (DEPTH, D_MODEL)),
        "mem_post_g": gain(ks[17], (DEPTH, D_MODEL)),
        "mem_wq": w(ks[18], (DEPTH, D_MODEL, MEM_W), D_MODEL),
        "mem_wk": w(ks[19], (DEPTH, D_MODEL, MEM_W), D_MODEL),
        "mem_wv": w(ks[20], (DEPTH, D_MODEL, MEM_W), D_MODEL),
        "mem_wo": w(ks[21], (DEPTH, MEM_W, D_MODEL), MEM_W),
        "mlp_pre_g": gain(ks[22], (DEPTH, D_MODEL)),
        "mlp_post_g": gain(ks[23], (DEPTH, D_MODEL)),
        "mlp_w_up": w(ks[24], (DEPTH, D_MODEL, D_FF), D_MODEL),
        "mlp_w_down": w(ks[25], (DEPTH, D_FF, D_MODEL), D_FF),
    }


def reference(x, mem, even_w_in, even_w_out, na_rpb, sw_sinks, odd_w_in, odd_w_out,
              diff_lam_q1, diff_lam_k1, diff_lam_q2, diff_lam_k2, diff_subln_g,
              mix_pre_g, mix_post_g, mem_norm_g, mem_pre_g, mem_post_g,
              mem_wq, mem_wk, mem_wv, mem_wo, mlp_pre_g, mlp_post_g, mlp_w_up, mlp_w_down):
    B, S, D = x.shape
    cos, sin = rope_tables(S)
    h = x
    for layer in range(DEPTH):
        hn = rms_norm(h, mix_pre_g[layer])
        if layer % 2 == 0:
            e = layer // 2
            proj = hn @ even_w_in[e]
            o0 = 0
            qa = proj[..., o0:o0 + NA_W]; o0 += NA_W
            ka = proj[..., o0:o0 + NA_W]; o0 += NA_W
            va = proj[..., o0:o0 + NA_W]; o0 += NA_W
            qs = proj[..., o0:o0 + SW_QW]; o0 += SW_QW
            kss = proj[..., o0:o0 + SW_KVW]; o0 += SW_KVW
            vs = proj[..., o0:o0 + SW_KVW]
            out_a = neighbourhood_attention(
                qa.reshape(B, S, NA_HEADS, HEAD_DIM), ka.reshape(B, S, NA_HEADS, HEAD_DIM),
                va.reshape(B, S, NA_HEADS, HEAD_DIM), na_rpb[e])
            out_b = sliding_window_gqa(
                qs.reshape(B, S, SW_Q_HEADS, HEAD_DIM), kss.reshape(B, S, SW_KV_HEADS, HEAD_DIM),
                vs.reshape(B, S, SW_KV_HEADS, HEAD_DIM), sw_sinks[e], cos, sin)
            y = jnp.concatenate([out_a.reshape(B, S, NA_W), out_b.reshape(B, S, SW_QW)], axis=-1)
            y = y @ even_w_out[e]
        else:
            o = layer // 2
            proj = hn @ odd_w_in[o]
            qd = proj[..., :DIFF_W].reshape(B, S, DIFF_HEADS, 2, HEAD_DIM)
            kd = proj[..., DIFF_W:2 * DIFF_W].reshape(B, S, DIFF_HEADS, 2, HEAD_DIM)
            vd = proj[..., 2 * DIFF_W:].reshape(B, S, DIFF_HEADS, 2 * HEAD_DIM)
            lambda_init = 0.8 - 0.6 * math.exp(-0.3 * layer)
            y = differential_attention(qd, kd, vd, diff_lam_q1[o], diff_lam_k1[o],
                                       diff_lam_q2[o], diff_lam_k2[o], diff_subln_g[o],
                                       lambda_init, cos, sin)
            y = y @ odd_w_out[o]
        h = h + rms_norm(y, mix_post_g[layer])
        mem_n = rms_norm(mem, mem_norm_g[layer])
        c = memory_cross_attention(rms_norm(h, mem_pre_g[layer]), mem_n, mem_wq[layer],
                                   mem_wk[layer], mem_wv[layer], mem_wo[layer])
        h = h + rms_norm(c, mem_post_g[layer])
        f = squared_relu_mlp(rms_norm(h, mlp_pre_g[layer]), mlp_w_up[layer], mlp_w_down[layer])
        h = h + rms_norm(f, mlp_post_g[layer])
    return h
```

```python
import functools
import math

import jax
import jax.numpy as jnp
from jax import lax
from jax.experimental import pallas as pl
from jax.experimental.pallas import tpu as pltpu

D_MODEL = 2048
SEQ = 8192
DEPTH = 2
GRID_W = 64
HEAD_DIM = 128
ROPE_THETA = 500000.0
ROPE_DIM = HEAD_DIM // 4
NORM_EPS = 1e-6
NA_HEADS = 8
NA_KH = 8
NA_KW = 16
SW_Q_HEADS = 8
SW_KV_HEADS = 2
SW_WINDOW = 128
DIFF_HEADS = 8
MEM_LEN = 256
MEM_HEADS = 4
D_FF = 4 * D_MODEL
NA_W = NA_HEADS * HEAD_DIM
SW_QW = SW_Q_HEADS * HEAD_DIM
SW_KVW = SW_KV_HEADS * HEAD_DIM
EVEN_IN = 3 * NA_W + SW_QW + 2 * SW_KVW
DIFF_W = DIFF_HEADS * 2 * HEAD_DIM
ODD_IN = 3 * DIFF_W
MEM_W = MEM_HEADS * HEAD_DIM

V7X_LANES = 128
V7X_VMEM_BYTES = 64 * 1024 * 1024

_F32 = jnp.float32
_BF16 = jnp.bfloat16
_LOG2E = 1.4426950408889634
_QK_SCALE = HEAD_DIM ** -0.5 * _LOG2E
_MASKED = -1e30
_NT = (((1,), (1,)), ((), ()))


def _nbytes(shape, dtype):
    return math.prod(shape) * jnp.dtype(dtype).itemsize


def _vmem_limit(buffers):
    need = sum(_nbytes(s, d) * n for s, d, n in buffers)
    return min(V7X_VMEM_BYTES - (4 << 20), need + (8 << 20))


def _rms(x, g):
    ms = jnp.mean(x * x, axis=-1, keepdims=True)
    return x * lax.rsqrt(ms + NORM_EPS) * g


def _rope_rot(a, c, s1, s2):
    return (a * c + pltpu.roll(a, V7X_LANES - ROPE_DIM // 2, 1) * s1
            + pltpu.roll(a, ROPE_DIM // 2, 1) * s2)


def _norm_matmul_kernel(x_ref, g_ref, w_ref, cs_ref, *rest, nchunk, rope_lo, rope_hi):
    if rope_lo is None:
        o_ref, hn_ref = rest
    else:
        c_ref, s1_ref, s2_ref, o_ref, hn_ref = rest
    j = pl.program_id(1)

    @pl.when(j == 0)
    def _():
        hn_ref[...] = _rms(x_ref[...], g_ref[...]).astype(_BF16)

    acc = jnp.dot(hn_ref[...], w_ref[...], preferred_element_type=_F32) * cs_ref[...]
    for c in range(nchunk):
        sl = slice(c * V7X_LANES, (c + 1) * V7X_LANES)
        a = acc[:, sl]
        if rope_lo is None:
            o_ref[:, sl] = a.astype(_BF16)
        else:
            gc = j * nchunk + c
            is_rope = jnp.logical_and(gc >= rope_lo, gc < rope_hi)

            @pl.when(is_rope)
            def _():
                o_ref[:, sl] = _rope_rot(a, c_ref[...], s1_ref[...], s2_ref[...]).astype(_BF16)

            @pl.when(jnp.logical_not(is_rope))
            def _():
                o_ref[:, sl] = a.astype(_BF16)


def _norm_matmul(x, g, w, colscale, rope_tabs=None, rope_cols=None, *, tm, tn, name):
    s, d = x.shape
    n = w.shape[1]
    nchunk = tn // V7X_LANES
    in_specs = [
        pl.BlockSpec((tm, d), lambda i, j: (i, 0)),
        pl.BlockSpec((1, d), lambda i, j: (0, 0)),
        pl.BlockSpec((d, tn), lambda i, j: (0, j)),
        pl.BlockSpec((1, tn), lambda i, j: (0, j)),
    ]
    args = [x, g, w, colscale]
    rope_lo = rope_hi = None
    if rope_cols is not None:
        rope_lo, rope_hi = rope_cols[0] // V7X_LANES, rope_cols[1] // V7X_LANES
        in_specs += [pl.BlockSpec((tm, V7X_LANES), lambda i, j: (i, 0))] * 3
        args += list(rope_tabs)
    bufs = [((tm, d), _F32, 3), ((d, tn), _BF16, 2), ((tm, tn), _F32, 2), ((tm, d), _BF16, 1)]
    return pl.pallas_call(
        functools.partial(_norm_matmul_kernel, nchunk=nchunk, rope_lo=rope_lo, rope_hi=rope_hi),
        out_shape=jax.ShapeDtypeStruct((s, n), _BF16),
        grid=(s // tm, n // tn),
        in_specs=in_specs,
        out_specs=pl.BlockSpec((tm, tn), lambda i, j: (i, j)),
        scratch_shapes=[pltpu.VMEM((tm, d), _BF16)],
        compiler_params=pltpu.CompilerParams(
            dimension_semantics=("parallel", "arbitrary"),
            vmem_limit_bytes=_vmem_limit(bufs)),
        name=name,
    )(*args)


def _matmul_norm_res_kernel(*refs, nparts):
    ys = refs[:nparts]
    ws = refs[nparts:2 * nparts]
    h_ref, g_ref, o_ref = refs[2 * nparts:]
    acc = jnp.dot(ys[0][...], ws[0][...], preferred_element_type=_F32)
    for y_ref, w_ref in zip(ys[1:], ws[1:]):
        acc = acc + jnp.dot(y_ref[...], w_ref[...], preferred_element_type=_F32)
    o_ref[...] = h_ref[...] + _rms(acc, g_ref[...])


def _matmul_norm_res(ys, ws, h, g, *, tm, name):
    s, d = h.shape
    nparts = len(ys)
    in_specs = [pl.BlockSpec((tm, y.shape[1]), lambda i: (i, 0)) for y in ys]
    in_specs += [pl.BlockSpec(w.shape, lambda i: (0, 0)) for w in ws]
    in_specs += [pl.BlockSpec((tm, d), lambda i: (i, 0)), pl.BlockSpec((1, d), lambda i: (0, 0))]
    bufs = [((tm, y.shape[1]), _BF16, 2) for y in ys] + [(w.shape, _BF16, 2) for w in ws]
    bufs += [((tm, d), _F32, 6)]
    return pl.pallas_call(
        functools.partial(_matmul_norm_res_kernel, nparts=nparts),
        out_shape=jax.ShapeDtypeStruct((s, d), _F32),
        grid=(s // tm,),
        in_specs=in_specs,
        out_specs=pl.BlockSpec((tm, d), lambda i: (i, 0)),
        compiler_params=pltpu.CompilerParams(
            dimension_semantics=("parallel",),
            vmem_limit_bytes=_vmem_limit(bufs)),
        name=name,
    )(*ys, *ws, h, g)


def _mem_attn_kernel(h_ref, gpre_ref, wq_ref, kv_ref, wo_ref, gpost_ref, o_ref):
    x = h_ref[...]
    hn = _rms(x, gpre_ref[...]).astype(_BF16)
    q = (jnp.dot(hn, wq_ref[...], preferred_element_type=_F32) * _QK_SCALE).astype(_BF16)
    outs = []
    for hd in range(MEM_HEADS):
        sl = slice(hd * HEAD_DIM, (hd + 1) * HEAD_DIM)
        k = kv_ref[:, sl]
        v = kv_ref[:, MEM_W + hd * HEAD_DIM:MEM_W + (hd + 1) * HEAD_DIM]
        sc = lax.dot_general(q[:, sl], k, _NT, preferred_element_type=_F32)
        m = jnp.max(sc, axis=-1, keepdims=True)
        p = jnp.exp2(sc - m)
        l = jnp.sum(p, axis=-1, keepdims=True)
        o = jnp.dot(p.astype(_BF16), v, preferred_element_type=_F32) / l
        outs.append(o.astype(_BF16))
    o_all = jnp.concatenate(outs, axis=-1)
    c = jnp.dot(o_all, wo_ref[...], preferred_element_type=_F32)
    o_ref[...] = x + _rms(c, gpost_ref[...])


def _mem_attn(h, gpre, wq, kv, wo, gpost, *, tm, name):
    s, d = h.shape
    row = lambda i: (i, 0)
    fixed = lambda i: (0, 0)
    bufs = [((tm, d), _F32, 6), ((tm, d), _BF16, 1), (wq.shape, _BF16, 2), (wo.shape, _BF16, 2),
            (kv.shape, _BF16, 2), ((tm, MEM_W), _F32, 4)]
    return pl.pallas_call(
        _mem_attn_kernel,
        out_shape=jax.ShapeDtypeStruct((s, d), _F32),
        grid=(s // tm,),
        in_specs=[pl.BlockSpec((tm, d), row), pl.BlockSpec((1, d), fixed),
                  pl.BlockSpec(wq.shape, fixed), pl.BlockSpec(kv.shape, fixed),
                  pl.BlockSpec(wo.shape, fixed), pl.BlockSpec((1, d), fixed)],
        out_specs=pl.BlockSpec((tm, d), row),
        compiler_params=pltpu.CompilerParams(
            dimension_semantics=("parallel",),
            vmem_limit_bytes=_vmem_limit(bufs)),
        name=name,
    )(h, gpre, wq, kv, wo, gpost)


def _mlp_kernel(h_ref, gpre_ref, wup_ref, wdn_ref, gpost_ref, o_ref, hn_ref, acc_ref):
    c = pl.program_id(1)

    @pl.when(c == 0)
    def _():
        hn_ref[...] = _rms(h_ref[...], gpre_ref[...]).astype(_BF16)

    u = jnp.maximum(jnp.dot(hn_ref[...], wup_ref[...], preferred_element_type=_F32), 0.0)
    part = jnp.dot((u * u).astype(_BF16), wdn_ref[...], preferred_element_type=_F32)

    @pl.when(c == 0)
    def _():
        acc_ref[...] = part

    @pl.when(c > 0)
    def _():
        acc_ref[...] += part

    @pl.when(c == pl.num_programs(1) - 1)
    def _():
        o_ref[...] = h_ref[...] + _rms(acc_ref[...], gpost_ref[...])


def _mlp(h, gpre, wup, wdn, gpost, *, tm, tf, name):
    s, d = h.shape
    f = wup.shape[1]
    bufs = [((tm, d), _F32, 5), ((tm, d), _BF16, 1), ((d, tf), _BF16, 4), ((tm, tf), _F32, 2)]
    return pl.pallas_call(
        _mlp_kernel,
        out_shape=jax.ShapeDtypeStruct((s, d), _F32),
        grid=(s // tm, f // tf),
        in_specs=[pl.BlockSpec((tm, d), lambda i, c: (i, 0)),
                  pl.BlockSpec((1, d), lambda i, c: (0, 0)),
                  pl.BlockSpec((d, tf), lambda i, c: (0, c)),
                  pl.BlockSpec((tf, d), lambda i, c: (c, 0)),
                  pl.BlockSpec((1, d), lambda i, c: (0, 0))],
        out_specs=pl.BlockSpec((tm, d), lambda i, c: (i, 0)),
        scratch_shapes=[pltpu.VMEM((tm, d), _BF16), pltpu.VMEM((tm, d), _F32)],
        compiler_params=pltpu.CompilerParams(
            dimension_semantics=("parallel", "arbitrary"),
            vmem_limit_bytes=_vmem_limit(bufs)),
        name=name,
    )(h, gpre, wup, wdn, gpost)


def _na_kernel(q_ref, k_ref, v_ref, b_ref, o_ref, *, rows_per_step, n_rows):
    rb = pl.program_id(1)
    band = NA_KH * GRID_W

    def body(r, carry):
        i = rb * rows_per_step + r
        rs = jnp.clip(i - NA_KH // 2, 0, n_rows - NA_KH)
        variant = rs - i + (NA_KH - 1)
        q0 = pl.multiple_of(r * GRID_W, GRID_W)
        k0 = pl.multiple_of(rs * GRID_W, GRID_W)
        q = q_ref[pl.ds(q0, GRID_W), :]
        kb = k_ref[pl.ds(k0, band), :]
        vb = v_ref[pl.ds(k0, band), :]
        sc = lax.dot_general(q, kb, _NT, preferred_element_type=_F32) + b_ref[variant, 0]
        m = jnp.max(sc, axis=-1, keepdims=True)
        p = jnp.exp2(sc - m)
        l = jnp.sum(p, axis=-1, keepdims=True)
        o = jnp.dot(p.astype(_BF16), vb, preferred_element_type=_F32) / l
        o_ref[pl.ds(q0, GRID_W), :] = o.astype(_BF16)
        return carry

    lax.fori_loop(0, rows_per_step, body, 0)


def _na_bias_table(rpb):
    col = jnp.arange(GRID_W)
    cs = jnp.clip(col - NA_KW // 2, 0, GRID_W - NA_KW)
    kc = jnp.arange(GRID_W)
    valid = (kc[None, :] >= cs[:, None]) & (kc[None, :] < cs[:, None] + NA_KW)
    coff = jnp.clip(kc[None, :] - col[:, None] + (NA_KW - 1), 0, 2 * NA_KW - 2)
    dense = jnp.where(valid[None, None], rpb[:, :, coff] * _LOG2E, _MASKED)
    ro = jnp.arange(NA_KH)[:, None] + jnp.arange(NA_KH)[None, :]
    tab = dense[:, ro]
    tab = jnp.transpose(tab, (1, 0, 3, 2, 4))
    return tab.reshape(NA_KH, NA_HEADS, GRID_W, NA_KH * GRID_W).astype(_F32)


def _na_attention(proj, bias_tab, *, rows_per_step, name):
    s = proj.shape[0]
    n_rows = s // GRID_W
    tq = rows_per_step * GRID_W
    kcol = NA_W // HEAD_DIM
    vcol = 2 * NA_W // HEAD_DIM
    bufs = [((s, HEAD_DIM), _BF16, 4), ((NA_KH, 1, GRID_W, NA_KH * GRID_W), _F32, 2),
            ((tq, HEAD_DIM), _BF16, 4)]
    return pl.pallas_call(
        functools.partial(_na_kernel, rows_per_step=rows_per_step, n_rows=n_rows),
        out_shape=jax.ShapeDtypeStruct((s, NA_W), _BF16),
        grid=(NA_HEADS, n_rows // rows_per_step),
        in_specs=[pl.BlockSpec((tq, HEAD_DIM), lambda h, rb: (rb, h)),
                  pl.BlockSpec((s, HEAD_DIM), lambda h, rb: (0, kcol + h)),
                  pl.BlockSpec((s, HEAD_DIM), lambda h, rb: (0, vcol + h)),
                  pl.BlockSpec((NA_KH, 1, GRID_W, NA_KH * GRID_W), lambda h, rb: (0, h, 0, 0))],
        out_specs=pl.BlockSpec((tq, HEAD_DIM), lambda h, rb: (rb, h)),
        compiler_params=pltpu.CompilerParams(
            dimension_semantics=("parallel", "arbitrary"),
            vmem_limit_bytes=_vmem_limit(bufs)),
        name=name,
    )(proj, proj, proj, bias_tab)


def _swa_kernel(sink_ref, q_ref, k_ref, v_ref, o_ref, *, tq, band, seq):
    kvh = pl.program_id(0)
    n = pl.program_id(1)
    group = SW_Q_HEADS // SW_KV_HEADS
    start = pl.multiple_of(jnp.clip(n * tq - SW_WINDOW, 0, seq - band), SW_WINDOW)
    kb = k_ref[pl.ds(start, band), :]
    vb = v_ref[pl.ds(start, band), :]
    qpos = n * tq + lax.broadcasted_iota(jnp.int32, (tq, band), 0)
    kpos = start + lax.broadcasted_iota(jnp.int32, (tq, band), 1)
    valid = jnp.abs(qpos - kpos) <= SW_WINDOW
    for g in range(group):
        sl = slice(g * HEAD_DIM, (g + 1) * HEAD_DIM)
        sink = sink_ref[kvh * group + g]
        sc = lax.dot_general(q_ref[:, sl], kb, _NT, preferred_element_type=_F32)
        sc = jnp.where(valid, sc, _MASKED)
        m = jnp.maximum(jnp.max(sc, axis=-1, keepdims=True), sink)
        p = jnp.exp2(sc - m)
        denom = jnp.sum(p, axis=-1, keepdims=True) + jnp.exp2(sink - m)
        o = jnp.dot(p.astype(_BF16), vb, preferred_element_type=_F32) / denom
        o_ref[:, sl] = o.astype(_BF16)


def _swa_attention(proj, sinks_log2, *, tq, name):
    s = proj.shape[0]
    group = SW_Q_HEADS // SW_KV_HEADS
    gw = group * HEAD_DIM
    band = tq + 2 * SW_WINDOW
    qcol = 3 * NA_W // gw
    kcol = (3 * NA_W + SW_QW) // HEAD_DIM
    vcol = (3 * NA_W + SW_QW + SW_KVW) // HEAD_DIM
    bufs = [((s, HEAD_DIM), _BF16, 4), ((tq, gw), _BF16, 4), ((tq, band), _F32, 6)]
    return pl.pallas_call(
        functools.partial(_swa_kernel, tq=tq, band=band, seq=s),
        out_shape=jax.ShapeDtypeStruct((s, SW_QW), _BF16),
        grid=(SW_KV_HEADS, s // tq),
        in_specs=[pl.BlockSpec(memory_space=pltpu.SMEM),
                  pl.BlockSpec((tq, gw), lambda kvh, n: (n, qcol + kvh)),
                  pl.BlockSpec((s, HEAD_DIM), lambda kvh, n: (0, kcol + kvh)),
                  pl.BlockSpec((s, HEAD_DIM), lambda kvh, n: (0, vcol + kvh))],
        out_specs=pl.BlockSpec((tq, gw), lambda kvh, n: (n, kvh)),
        compiler_params=pltpu.CompilerParams(
            dimension_semantics=("parallel", "arbitrary"),
            vmem_limit_bytes=_vmem_limit(bufs)),
        name=name,
    )(sinks_log2, proj, proj, proj)


def _diff_kernel(q_ref, k_ref, v_ref, lq1_ref, lk1_ref, lq2_ref, lk2_ref, g_ref, o_ref,
                 m_ref, l_ref, acc_ref, *, tk, seq, lambda_init):
    tq = q_ref.shape[0]
    m_ref[...] = jnp.full(m_ref.shape, -jnp.inf, _F32)
    l_ref[...] = jnp.zeros(l_ref.shape, _F32)
    acc_ref[...] = jnp.zeros(acc_ref.shape, _F32)

    def step(kb, carry):
        k0 = pl.multiple_of(kb * tk, tk)
        v = v_ref[pl.ds(k0, tk), :]
        for t in range(2):
            sl = slice(t * HEAD_DIM, (t + 1) * HEAD_DIM)
            k = k_ref[pl.ds(k0, tk), sl]
            sc = lax.dot_general(q_ref[:, sl], k, _NT, preferred_element_type=_F32)
            m_prev = m_ref[t]
            m_new = jnp.maximum(m_prev, jnp.max(sc, axis=-1, keepdims=True))
            alpha = jnp.exp2(m_prev - m_new)
            p = jnp.exp2(sc - m_new)
            l_ref[t] = alpha * l_ref[t] + jnp.sum(p, axis=-1, keepdims=True)
            acc_ref[t] = alpha * acc_ref[t] + jnp.dot(p.astype(_BF16), v, preferred_element_type=_F32)
            m_ref[t] = m_new
        return carry

    lax.fori_loop(0, seq // tk, step, 0)

    lam = (jnp.exp(jnp.sum(lq1_ref[...] * lk1_ref[...], axis=-1, keepdims=True))
           - jnp.exp(jnp.sum(lq2_ref[...] * lk2_ref[...], axis=-1, keepdims=True)) + lambda_init)
    o = acc_ref[0] / l_ref[0] - lam * (acc_ref[1] / l_ref[1])
    o_ref[...] = (_rms(o, g_ref[...]) * (1.0 - lambda_init)).astype(_BF16)


def _diff_attention(proj, lq1, lk1, lq2, lk2, subln_g, lambda_init, *, tq, tk, name):
    s = proj.shape[0]
    hw = 2 * HEAD_DIM
    kcol = DIFF_W // hw
    vcol = 2 * DIFF_W // hw
    vec = pl.BlockSpec((1, HEAD_DIM), lambda h, qb: (0, 0))
    bufs = [((s, hw), _BF16, 4), ((tq, hw), _BF16, 4), ((tq, tk), _F32, 6),
            ((2, tq, hw), _F32, 2), ((2, tq, V7X_LANES), _F32, 2)]
    return pl.pallas_call(
        functools.partial(_diff_kernel, tk=tk, seq=s, lambda_init=lambda_init),
        out_shape=jax.ShapeDtypeStruct((s, DIFF_W), _BF16),
        grid=(DIFF_HEADS, s // tq),
        in_specs=[pl.BlockSpec((tq, hw), lambda h, qb: (qb, h)),
                  pl.BlockSpec((s, hw), lambda h, qb: (0, kcol + h)),
                  pl.BlockSpec((s, hw), lambda h, qb: (0, vcol + h)),
                  vec, vec, vec, vec,
                  pl.BlockSpec((1, hw), lambda h, qb: (0, 0))],
        out_specs=pl.BlockSpec((tq, hw), lambda h, qb: (qb, h)),
        scratch_shapes=[pltpu.VMEM((2, tq, 1), _F32), pltpu.VMEM((2, tq, 1), _F32),
                        pltpu.VMEM((2, tq, hw), _F32)],
        compiler_params=pltpu.CompilerParams(
            dimension_semantics=("parallel", "arbitrary"),
            vmem_limit_bytes=_vmem_limit(bufs)),
        name=name,
    )(proj, proj, proj, lq1, lk1, lq2, lk2, subln_g)


def _rope_tables(seq):
    half = ROPE_DIM // 2
    inv = 1.0 / (ROPE_THETA ** (jnp.arange(0, ROPE_DIM, 2, dtype=_F32) / ROPE_DIM))
    ang = jnp.arange(seq, dtype=_F32)[:, None] * inv[None, :]
    cos, sin = jnp.cos(ang), jnp.sin(ang)
    pad = V7X_LANES - ROPE_DIM
    c = jnp.concatenate([cos, cos, jnp.ones((seq, pad), _F32)], axis=1)
    s1 = jnp.concatenate([-sin, jnp.zeros((seq, pad + half), _F32)], axis=1)
    s2 = jnp.concatenate([jnp.zeros((seq, half), _F32), sin, jnp.zeros((seq, pad), _F32)], axis=1)
    return c, s1, s2


def _col_scale(n, ranges):
    cs = jnp.ones((1, n), _F32)
    for lo, hi in ranges:
        cs = cs.at[:, lo:hi].set(_QK_SCALE)
    return cs


def kernel(x, mem, even_w_in, even_w_out, na_rpb, sw_sinks, odd_w_in, odd_w_out, diff_lam_q1, diff_lam_k1, diff_lam_q2, diff_lam_k2, diff_subln_g, mix_pre_g, mix_post_g, mem_norm_g, mem_pre_g, mem_post_g, mem_wq, mem_wk, mem_wv, mem_wo, mlp_pre_g, mlp_post_g, mlp_w_up, mlp_w_down):
    b, s, d = x.shape
    assert (b, s, d) == (1, SEQ, D_MODEL)
    h = x.reshape(s, d)
    mem2 = mem.reshape(MEM_LEN, d)
    rope_tabs = _rope_tables(s)
    row = lambda v: v.reshape(1, -1)

    for layer in range(DEPTH):
        if layer % 2 == 0:
            e = layer // 2
            sw_q0 = 3 * NA_W
            proj = _norm_matmul(
                h, row(mix_pre_g[layer]), even_w_in[e].astype(_BF16),
                _col_scale(EVEN_IN, [(0, NA_W), (sw_q0, sw_q0 + SW_QW)]),
                rope_tabs, (sw_q0, sw_q0 + SW_QW + SW_KVW), tm=1024, tn=512, name=f"in_proj{layer}")
            out_a = _na_attention(proj, _na_bias_table(na_rpb[e]), rows_per_step=8, name=f"na{layer}")
            out_b = _swa_attention(proj, sw_sinks[e] * _LOG2E, tq=128, name=f"swa{layer}")
            w_out = even_w_out[e].astype(_BF16)
            h = _matmul_norm_res([out_a, out_b], [w_out[:NA_W], w_out[NA_W:]], h,
                                 row(mix_post_g[layer]), tm=512, name=f"out_proj{layer}")
        else:
            o = layer // 2
            lambda_init = 0.8 - 0.6 * math.exp(-0.3 * layer)
            proj = _norm_matmul(
                h, row(mix_pre_g[layer]), odd_w_in[o].astype(_BF16),
                _col_scale(ODD_IN, [(0, DIFF_W)]),
                rope_tabs, (0, 2 * DIFF_W), tm=1024, tn=512, name=f"in_proj{layer}")
            y = _diff_attention(proj, row(diff_lam_q1[o]), row(diff_lam_k1[o]), row(diff_lam_q2[o]),
                                row(diff_lam_k2[o]), row(diff_subln_g[o]), lambda_init,
                                tq=512, tk=512, name=f"diff{layer}")
            h = _matmul_norm_res([y], [odd_w_out[o].astype(_BF16)], h, row(mix_post_g[layer]),
                                 tm=512, name=f"out_proj{layer}")

        wkv = jnp.concatenate([mem_wk[layer], mem_wv[layer]], axis=1).astype(_BF16)
        kv = _norm_matmul(mem2, row(mem_norm_g[layer]), wkv, jnp.ones((1, 2 * MEM_W), _F32),
                          tm=MEM_LEN, tn=512, name=f"mem_kv{layer}")
        h = _mem_attn(h, row(mem_pre_g[layer]), mem_wq[layer].astype(_BF16), kv,
                      mem_wo[layer].astype(_BF16), row(mem_post_g[layer]), tm=512, name=f"mem_attn{layer}")
        h = _mlp(h, row(mlp_pre_g[layer]), mlp_w_up[layer].astype(_BF16), mlp_w_down[layer].astype(_BF16),
                 row(mlp_post_g[layer]), tm=512, tf=1024, name=f"mlp{layer}")
    return h.reshape(b, s, d)
```

```python
import functools
import math

import jax
import jax.numpy as jnp
import numpy as np
from jax import lax
from jax.experimental import pallas as pl
from jax.experimental.pallas import tpu as pltpu

D_MODEL = 2048
SEQ = 8192
DEPTH = 2
GRID_W = 64
HEAD_DIM = 128
ROPE_THETA = 500000.0
ROPE_DIM = HEAD_DIM // 4
NORM_EPS = 1e-6
NA_HEADS = 8
NA_KH = 8
NA_KW = 16
SW_Q_HEADS = 8
SW_KV_HEADS = 2
SW_WINDOW = 128
DIFF_HEADS = 8
MEM_LEN = 256
MEM_HEADS = 4
D_FF = 4 * D_MODEL
NA_W = NA_HEADS * HEAD_DIM
SW_QW = SW_Q_HEADS * HEAD_DIM
SW_KVW = SW_KV_HEADS * HEAD_DIM
EVEN_IN = 3 * NA_W + SW_QW + 2 * SW_KVW
DIFF_W = DIFF_HEADS * 2 * HEAD_DIM
ODD_IN = 3 * DIFF_W
MEM_W = MEM_HEADS * HEAD_DIM

V7X_LANES = 128
V7X_VMEM_BYTES = 64 * 1024 * 1024

_F32 = jnp.float32
_BF16 = jnp.bfloat16
_LOG2E = 1.4426950408889634
_QK_SCALE = HEAD_DIM ** -0.5 * _LOG2E
_MASKED = -1e30
_NT = (((1,), (1,)), ((), ()))


def _nbytes(shape, dtype):
    return math.prod(shape) * jnp.dtype(dtype).itemsize


def _vmem_limit(buffers):
    need = sum(_nbytes(s, d) * n for s, d, n in buffers)
    return min(V7X_VMEM_BYTES - (4 << 20), need + (8 << 20))


def _rms(x, g):
    ms = jnp.mean(x * x, axis=-1, keepdims=True)
    return x * lax.rsqrt(ms + NORM_EPS) * g


def _rope_rot(a, c, s1, s2):
    return (a * c + pltpu.roll(a, V7X_LANES - ROPE_DIM // 2, 1) * s1
            + pltpu.roll(a, ROPE_DIM // 2, 1) * s2)


def _norm_matmul_kernel(x_ref, g_ref, w_ref, cs_ref, *rest, nchunk, rope_lo, rope_hi):
    if rope_lo is None:
        o_ref, hn_ref = rest
    else:
        c_ref, s1_ref, s2_ref, o_ref, hn_ref = rest
    j = pl.program_id(1)

    @pl.when(j == 0)
    def _():
        hn_ref[...] = _rms(x_ref[...], g_ref[...]).astype(_BF16)

    acc = jnp.dot(hn_ref[...], w_ref[...], preferred_element_type=_F32) * cs_ref[...]
    for c in range(nchunk):
        sl = slice(c * V7X_LANES, (c + 1) * V7X_LANES)
        a = acc[:, sl]
        if rope_lo is None:
            o_ref[:, sl] = a.astype(_BF16)
        else:
            gc = j * nchunk + c
            is_rope = jnp.logical_and(gc >= rope_lo, gc < rope_hi)

            @pl.when(is_rope)
            def _():
                o_ref[:, sl] = _rope_rot(a, c_ref[...], s1_ref[...], s2_ref[...]).astype(_BF16)

            @pl.when(jnp.logical_not(is_rope))
            def _():
                o_ref[:, sl] = a.astype(_BF16)


def _norm_matmul(x, g, w, colscale, rope_tabs=None, rope_cols=None, *, tm, tn, name):
    s, d = x.shape
    n = w.shape[1]
    nchunk = tn // V7X_LANES
    in_specs = [
        pl.BlockSpec((tm, d), lambda i, j: (i, 0)),
        pl.BlockSpec((1, d), lambda i, j: (0, 0)),
        pl.BlockSpec((d, tn), lambda i, j: (0, j)),
        pl.BlockSpec((1, tn), lambda i, j: (0, j)),
    ]
    args = [x, g, w, colscale]
    rope_lo = rope_hi = None
    if rope_cols is not None:
        rope_lo, rope_hi = rope_cols[0] // V7X_LANES, rope_cols[1] // V7X_LANES
        in_specs += [pl.BlockSpec((tm, V7X_LANES), lambda i, j: (i, 0))] * 3
        args += list(rope_tabs)
    bufs = [((tm, d), _F32, 3), ((d, tn), _BF16, 2), ((tm, tn), _F32, 2), ((tm, d), _BF16, 1)]
    return pl.pallas_call(
        functools.partial(_norm_matmul_kernel, nchunk=nchunk, rope_lo=rope_lo, rope_hi=rope_hi),
        out_shape=jax.ShapeDtypeStruct((s, n), _BF16),
        grid=(s // tm, n // tn),
        in_specs=in_specs,
        out_specs=pl.BlockSpec((tm, tn), lambda i, j: (i, j)),
        scratch_shapes=[pltpu.VMEM((tm, d), _BF16)],
        compiler_params=pltpu.CompilerParams(
            dimension_semantics=("parallel", "arbitrary"),
            vmem_limit_bytes=_vmem_limit(bufs)),
        name=name,
    )(*args)


def _matmul_norm_res_kernel(*refs, nparts):
    ys = refs[:nparts]
    ws = refs[nparts:2 * nparts]
    h_ref, g_ref, o_ref = refs[2 * nparts:]
    acc = jnp.dot(ys[0][...], ws[0][...], preferred_element_type=_F32)
    for y_ref, w_ref in zip(ys[1:], ws[1:]):
        acc = acc + jnp.dot(y_ref[...], w_ref[...], preferred_element_type=_F32)
    o_ref[...] = h_ref[...] + _rms(acc, g_ref[...])


def _matmul_norm_res(ys, ws, h, g, *, tm, name):
    s, d = h.shape
    nparts = len(ys)
    in_specs = [pl.BlockSpec((tm, y.shape[1]), lambda i: (i, 0)) for y in ys]
    in_specs += [pl.BlockSpec(w.shape, lambda i: (0, 0)) for w in ws]
    in_specs += [pl.BlockSpec((tm, d), lambda i: (i, 0)), pl.BlockSpec((1, d), lambda i: (0, 0))]
    bufs = [((tm, y.shape[1]), _BF16, 2) for y in ys] + [(w.shape, _BF16, 2) for w in ws]
    bufs += [((tm, d), _F32, 6)]
    return pl.pallas_call(
        functools.partial(_matmul_norm_res_kernel, nparts=nparts),
        out_shape=jax.ShapeDtypeStruct((s, d), _F32),
        grid=(s // tm,),
        in_specs=in_specs,
        out_specs=pl.BlockSpec((tm, d), lambda i: (i, 0)),
        compiler_params=pltpu.CompilerParams(
            dimension_semantics=("parallel",),
            vmem_limit_bytes=_vmem_limit(bufs)),
        name=name,
    )(*ys, *ws, h, g)


def _mem_attn_kernel(h_ref, gpre_ref, wq_ref, kv_ref, wo_ref, gpost_ref, o_ref):
    x = h_ref[...]
    hn = _rms(x, gpre_ref[...]).astype(_BF16)
    q = (jnp.dot(hn, wq_ref[...], preferred_element_type=_F32) * _QK_SCALE).astype(_BF16)
    outs = []
    for hd in range(MEM_HEADS):
        sl = slice(hd * HEAD_DIM, (hd + 1) * HEAD_DIM)
        k = kv_ref[:, sl]
        v = kv_ref[:, MEM_W + hd * HEAD_DIM:MEM_W + (hd + 1) * HEAD_DIM]
        sc = lax.dot_general(q[:, sl], k, _NT, preferred_element_type=_F32)
        m = jnp.max(sc, axis=-1, keepdims=True)
        p = jnp.exp2(sc - m)
        l = jnp.sum(p, axis=-1, keepdims=True)
        o = jnp.dot(p.astype(_BF16), v, preferred_element_type=_F32) / l
        outs.append(o.astype(_BF16))
    o_all = jnp.concatenate(outs, axis=-1)
    c = jnp.dot(o_all, wo_ref[...], preferred_element_type=_F32)
    o_ref[...] = x + _rms(c, gpost_ref[...])


def _mem_attn(h, gpre, wq, kv, wo, gpost, *, tm, name):
    s, d = h.shape
    row = lambda i: (i, 0)
    fixed = lambda i: (0, 0)
    bufs = [((tm, d), _F32, 6), ((tm, d), _BF16, 1), (wq.shape, _BF16, 2), (wo.shape, _BF16, 2),
            (kv.shape, _BF16, 2), ((tm, MEM_W), _F32, 4)]
    return pl.pallas_call(
        _mem_attn_kernel,
        out_shape=jax.ShapeDtypeStruct((s, d), _F32),
        grid=(s // tm,),
        in_specs=[pl.BlockSpec((tm, d), row), pl.BlockSpec((1, d), fixed),
                  pl.BlockSpec(wq.shape, fixed), pl.BlockSpec(kv.shape, fixed),
                  pl.BlockSpec(wo.shape, fixed), pl.BlockSpec((1, d), fixed)],
        out_specs=pl.BlockSpec((tm, d), row),
        compiler_params=pltpu.CompilerParams(
            dimension_semantics=("parallel",),
            vmem_limit_bytes=_vmem_limit(bufs)),
        name=name,
    )(h, gpre, wq, kv, wo, gpost)


def _mlp_kernel(h_ref, gpre_ref, wup_ref, wdn_ref, gpost_ref, o_ref, hn_ref, acc_ref):
    c = pl.program_id(1)

    @pl.when(c == 0)
    def _():
        hn_ref[...] = _rms(h_ref[...], gpre_ref[...]).astype(_BF16)

    u = jnp.maximum(jnp.dot(hn_ref[...], wup_ref[...], preferred_element_type=_F32), 0.0)
    part = jnp.dot((u * u).astype(_BF16), wdn_ref[...], preferred_element_type=_F32)

    @pl.when(c == 0)
    def _():
        acc_ref[...] = part

    @pl.when(c > 0)
    def _():
        acc_ref[...] += part

    @pl.when(c == pl.num_programs(1) - 1)
    def _():
        o_ref[...] = h_ref[...] + _rms(acc_ref[...], gpost_ref[...])


def _mlp(h, gpre, wup, wdn, gpost, *, tm, tf, name):
    s, d = h.shape
    f = wup.shape[1]
    bufs = [((tm, d), _F32, 5), ((tm, d), _BF16, 1), ((d, tf), _BF16, 4), ((tm, tf), _F32, 2)]
    return pl.pallas_call(
        _mlp_kernel,
        out_shape=jax.ShapeDtypeStruct((s, d), _F32),
        grid=(s // tm, f // tf),
        in_specs=[pl.BlockSpec((tm, d), lambda i, c: (i, 0)),
                  pl.BlockSpec((1, d), lambda i, c: (0, 0)),
                  pl.BlockSpec((d, tf), lambda i, c: (0, c)),
                  pl.BlockSpec((tf, d), lambda i, c: (c, 0)),
                  pl.BlockSpec((1, d), lambda i, c: (0, 0))],
        out_specs=pl.BlockSpec((tm, d), lambda i, c: (i, 0)),
        scratch_shapes=[pltpu.VMEM((tm, d), _BF16), pltpu.VMEM((tm, d), _F32)],
        compiler_params=pltpu.CompilerParams(
            dimension_semantics=("parallel", "arbitrary"),
            vmem_limit_bytes=_vmem_limit(bufs)),
        name=name,
    )(h, gpre, wup, wdn, gpost)


NA_QROWS = 4
NA_BAND = NA_QROWS + NA_KH


def _na_kernel(q_ref, k_ref, vt_ref, b_ref, o_ref, *, blocks_per_step, n_rows):
    step = pl.program_id(1)
    tq = NA_QROWS * GRID_W
    band = NA_BAND * GRID_W
    n_blocks = n_rows // NA_QROWS
    scs, vts = [], []
    for u in range(blocks_per_step):
        b = step * blocks_per_step + u
        variant = jnp.where(b == 0, 0, jnp.where(b == n_blocks - 1, 2, 1))
        row0 = jnp.clip(b * NA_QROWS - NA_KH // 2, 0, n_rows - NA_BAND)
        k0 = pl.multiple_of(row0 * GRID_W, NA_QROWS * GRID_W)
        sc = lax.dot_general(k_ref[pl.ds(k0, band), :], q_ref[u * tq:(u + 1) * tq, :], _NT,
                             preferred_element_type=_F32)
        scs.append(sc + b_ref[variant, 0])
        vts.append(vt_ref[:, pl.ds(k0, band)])
    ps, ls = [], []
    for sc in scs:
        m = jnp.max(sc, axis=0, keepdims=True)
        p = jnp.exp2(sc - m)
        ls.append(jnp.sum(p, axis=0, keepdims=True))
        ps.append(p.astype(_BF16))
    for u in range(blocks_per_step):
        ot = jnp.dot(vts[u], ps[u], preferred_element_type=_F32) / ls[u]
        o_ref[u * tq:(u + 1) * tq, :] = ot.T.astype(_BF16)


def _na_bias_table(rpb, n_rows):
    col = np.arange(GRID_W)
    cs = np.clip(col - NA_KW // 2, 0, GRID_W - NA_KW)
    kc = np.arange(GRID_W)
    col_ok = (kc[:, None] >= cs[None, :]) & (kc[:, None] < cs[None, :] + NA_KW)
    coff = np.clip(kc[:, None] - col[None, :] + (NA_KW - 1), 0, 2 * NA_KW - 2)
    tabs = []
    for i0 in (0, 2 * NA_QROWS, n_rows - NA_QROWS):
        row0 = int(np.clip(i0 - NA_KH // 2, 0, n_rows - NA_BAND))
        qi = i0 + np.arange(NA_QROWS)
        rs = np.clip(qi - NA_KH // 2, 0, n_rows - NA_KH)
        krow = row0 + np.arange(NA_BAND)
        row_ok = (krow[:, None] >= rs[None, :]) & (krow[:, None] < rs[None, :] + NA_KH)
        roff = np.clip(krow[:, None] - qi[None, :] + (NA_KH - 1), 0, 2 * NA_KH - 2)
        shape = (NA_BAND, GRID_W, NA_QROWS, GRID_W)
        ok = np.broadcast_to(row_ok[:, None, :, None] & col_ok[None, :, None, :], shape)
        ro = np.broadcast_to(roff[:, None, :, None], shape)
        co = np.broadcast_to(coff[None, :, None, :], shape)
        t = jnp.where(ok[None], rpb[:, ro, co] * _LOG2E, _MASKED)
        tabs.append(t.reshape(NA_HEADS, NA_BAND * GRID_W, NA_QROWS * GRID_W))
    return jnp.stack(tabs).astype(_F32)


def _na_attention(proj, vt, bias_tab, *, blocks_per_step, name):
    s = proj.shape[0]
    n_rows = s // GRID_W
    tq = blocks_per_step * NA_QROWS * GRID_W
    kcol = NA_W // HEAD_DIM
    bshape = (3, 1, NA_BAND * GRID_W, NA_QROWS * GRID_W)
    bufs = [((s, HEAD_DIM), _BF16, 4), (bshape, _F32, 2), ((tq, HEAD_DIM), _BF16, 4),
            ((NA_BAND * GRID_W, NA_QROWS * GRID_W), _F32, 3 * blocks_per_step)]
    return pl.pallas_call(
        functools.partial(_na_kernel, blocks_per_step=blocks_per_step, n_rows=n_rows),
        out_shape=jax.ShapeDtypeStruct((s, NA_W), _BF16),
        grid=(NA_HEADS, n_rows // (NA_QROWS * blocks_per_step)),
        in_specs=[pl.BlockSpec((tq, HEAD_DIM), lambda h, i: (i, h)),
                  pl.BlockSpec((s, HEAD_DIM), lambda h, i: (0, kcol + h)),
                  pl.BlockSpec((HEAD_DIM, s), lambda h, i: (h, 0)),
                  pl.BlockSpec(bshape, lambda h, i: (0, h, 0, 0))],
        out_specs=pl.BlockSpec((tq, HEAD_DIM), lambda h, i: (i, h)),
        compiler_params=pltpu.CompilerParams(
            dimension_semantics=("parallel", "arbitrary"),
            vmem_limit_bytes=_vmem_limit(bufs)),
        name=name,
    )(proj, proj, vt, bias_tab)


def _swa_kernel(sink_ref, q_ref, k_ref, vt_ref, b_ref, o_ref, *, tq, band, seq):
    n = pl.program_id(0)
    group = SW_Q_HEADS // SW_KV_HEADS
    start = pl.multiple_of(jnp.clip(n * tq - SW_WINDOW, 0, seq - band), SW_WINDOW)
    bias = b_ref[0]
    scs = []
    for kvh in range(SW_KV_HEADS):
        kb = k_ref[pl.ds(start, band), kvh * HEAD_DIM:(kvh + 1) * HEAD_DIM]
        q = jnp.concatenate([q_ref[:, (kvh * group + g) * HEAD_DIM:(kvh * group + g + 1) * HEAD_DIM]
                             for g in range(group)], axis=0)
        scs.append(lax.dot_general(kb, q, _NT, preferred_element_type=_F32))
    ps, denoms = [], []
    for kvh in range(SW_KV_HEADS):
        pg, dg = [], []
        for g in range(group):
            sink = sink_ref[kvh * group + g]
            sc = scs[kvh][:, g * tq:(g + 1) * tq] + bias
            m = jnp.maximum(jnp.max(sc, axis=0, keepdims=True), sink)
            p = jnp.exp2(sc - m)
            dg.append(jnp.sum(p, axis=0, keepdims=True) + jnp.exp2(sink - m))
            pg.append(p.astype(_BF16))
        ps.append(jnp.concatenate(pg, axis=1))
        denoms.append(jnp.concatenate(dg, axis=1))
    for kvh in range(SW_KV_HEADS):
        vt = vt_ref[kvh * HEAD_DIM:(kvh + 1) * HEAD_DIM, pl.ds(start, band)]
        ot = jnp.dot(vt, ps[kvh], preferred_element_type=_F32) / denoms[kvh]
        for g in range(group):
            hd = kvh * group + g
            o_ref[:, hd * HEAD_DIM:(hd + 1) * HEAD_DIM] = ot[:, g * tq:(g + 1) * tq].T.astype(_BF16)


def _swa_bias_table(tq, band):
    kr = np.arange(band)[:, None]
    qc = np.arange(tq)[None, :]
    tabs = [np.where(np.abs(qc - (rel + kr)) <= SW_WINDOW, 0.0, _MASKED)
            for rel in (0, -SW_WINDOW, tq - band)]
    return np.stack(tabs).astype(np.float32)


def _swa_attention(proj, vt, sinks_log2, *, tq, name):
    s = proj.shape[0]
    band = tq + 2 * SW_WINDOW
    n_blocks = s // tq
    qcol = 3 * NA_W // SW_QW
    kcol = (3 * NA_W + SW_QW) // SW_KVW
    rows = SW_Q_HEADS // SW_KV_HEADS * tq
    bufs = [((s, SW_KVW), _BF16, 4), ((tq, SW_QW), _BF16, 4), ((band, rows), _F32, 8)]
    variant = lambda n: jnp.where(n == 0, 0, jnp.where(n == n_blocks - 1, 2, 1))
    return pl.pallas_call(
        functools.partial(_swa_kernel, tq=tq, band=band, seq=s),
        out_shape=jax.ShapeDtypeStruct((s, SW_QW), _BF16),
        grid=(n_blocks,),
        in_specs=[pl.BlockSpec(memory_space=pltpu.SMEM),
                  pl.BlockSpec((tq, SW_QW), lambda n: (n, qcol)),
                  pl.BlockSpec((s, SW_KVW), lambda n: (0, kcol)),
                  pl.BlockSpec((SW_KVW, s), lambda n: (0, 0)),
                  pl.BlockSpec((1, band, tq), lambda n: (variant(n), 0, 0))],
        out_specs=pl.BlockSpec((tq, SW_QW), lambda n: (n, 0)),
        compiler_params=pltpu.CompilerParams(
            dimension_semantics=("arbitrary",),
            vmem_limit_bytes=_vmem_limit(bufs)),
        name=name,
    )(sinks_log2, proj, proj, vt, _swa_bias_table(tq, band))


def _diff_kernel(q_ref, k_ref, vt_ref, lq1_ref, lk1_ref, lq2_ref, lk2_ref, g_ref, o_ref,
                 m_ref, l_ref, acc_ref, s_ref, *, tk, seq, lambda_init):
    m_ref[...] = jnp.full(m_ref.shape, -jnp.inf, _F32)
    l_ref[...] = jnp.zeros(l_ref.shape, _F32)
    acc_ref[...] = jnp.zeros(acc_ref.shape, _F32)
    nblk = seq // tk

    def scores(kb, slot):
        k0 = pl.multiple_of(kb * tk, tk)
        for t in range(2):
            sl = slice(t * HEAD_DIM, (t + 1) * HEAD_DIM)
            s_ref[slot + t] = lax.dot_general(k_ref[pl.ds(k0, tk), sl], q_ref[:, sl], _NT,
                                              preferred_element_type=_F32)

    def update(kb, slot):
        k0 = pl.multiple_of(kb * tk, tk)
        vt = vt_ref[:, pl.ds(k0, tk)]
        for t in range(2):
            sc = s_ref[slot + t]
            m_prev = m_ref[t]
            m_new = jnp.maximum(m_prev, jnp.max(sc, axis=0, keepdims=True))
            alpha = jnp.exp2(m_prev - m_new)
            p = jnp.exp2(sc - m_new)
            l_ref[t] = alpha * l_ref[t] + jnp.sum(p, axis=0, keepdims=True)
            acc_ref[t] = alpha * acc_ref[t] + jnp.dot(vt, p.astype(_BF16), preferred_element_type=_F32)
            m_ref[t] = m_new

    scores(0, 0)

    def pair(i, carry):
        scores(2 * i + 1, 2)
        update(2 * i, 0)
        scores(2 * i + 2, 0)
        update(2 * i + 1, 2)
        return carry

    lax.fori_loop(0, nblk // 2 - 1, pair, 0)
    scores(nblk - 1, 2)
    update(nblk - 2, 0)
    update(nblk - 1, 2)

    lam = (jnp.exp(jnp.sum(lq1_ref[...] * lk1_ref[...], axis=-1, keepdims=True))
           - jnp.exp(jnp.sum(lq2_ref[...] * lk2_ref[...], axis=-1, keepdims=True)) + lambda_init)
    ot = acc_ref[0] / l_ref[0] - lam * (acc_ref[1] / l_ref[1])
    o_ref[...] = (_rms(ot.T, g_ref[...]) * (1.0 - lambda_init)).astype(_BF16)


def _diff_attention(proj, vt, lq1, lk1, lq2, lk2, subln_g, lambda_init, *, tq, tk, name):
    s = proj.shape[0]
    hw = 2 * HEAD_DIM
    kcol = DIFF_W // hw
    vec = pl.BlockSpec((1, HEAD_DIM), lambda h, qb: (0, 0))
    bufs = [((s, hw), _BF16, 4), ((tq, hw), _BF16, 4), ((tk, tq), _F32, 6),
            ((2, hw, tq), _F32, 2)]
    return pl.pallas_call(
        functools.partial(_diff_kernel, tk=tk, seq=s, lambda_init=lambda_init),
        out_shape=jax.ShapeDtypeStruct((s, DIFF_W), _BF16),
        grid=(DIFF_HEADS, s // tq),
        in_specs=[pl.BlockSpec((tq, hw), lambda h, qb: (qb, h)),
                  pl.BlockSpec((s, hw), lambda h, qb: (0, kcol + h)),
                  pl.BlockSpec((hw, s), lambda h, qb: (h, 0)),
                  vec, vec, vec, vec,
                  pl.BlockSpec((1, hw), lambda h, qb: (0, 0))],
        out_specs=pl.BlockSpec((tq, hw), lambda h, qb: (qb, h)),
        scratch_shapes=[pltpu.VMEM((2, 1, tq), _F32), pltpu.VMEM((2, 1, tq), _F32),
                        pltpu.VMEM((2, hw, tq), _F32), pltpu.VMEM((4, tk, tq), _F32)],
        compiler_params=pltpu.CompilerParams(
            dimension_semantics=("parallel", "arbitrary"),
            vmem_limit_bytes=_vmem_limit(bufs)),
        name=name,
    )(proj, proj, vt, lq1, lk1, lq2, lk2, subln_g)


def _rope_tables(seq):
    half = ROPE_DIM // 2
    inv = 1.0 / (ROPE_THETA ** (jnp.arange(0, ROPE_DIM, 2, dtype=_F32) / ROPE_DIM))
    ang = jnp.arange(seq, dtype=_F32)[:, None] * inv[None, :]
    cos, sin = jnp.cos(ang), jnp.sin(ang)
    pad = V7X_LANES - ROPE_DIM
    c = jnp.concatenate([cos, cos, jnp.ones((seq, pad), _F32)], axis=1)
    s1 = jnp.concatenate([-sin, jnp.zeros((seq, pad + half), _F32)], axis=1)
    s2 = jnp.concatenate([jnp.zeros((seq, half), _F32), sin, jnp.zeros((seq, pad), _F32)], axis=1)
    return c, s1, s2


def _col_scale(n, ranges):
    cs = jnp.ones((1, n), _F32)
    for lo, hi in ranges:
        cs = cs.at[:, lo:hi].set(_QK_SCALE)
    return cs


def kernel(x, mem, even_w_in, even_w_out, na_rpb, sw_sinks, odd_w_in, odd_w_out, diff_lam_q1, diff_lam_k1, diff_lam_q2, diff_lam_k2, diff_subln_g, mix_pre_g, mix_post_g, mem_norm_g, mem_pre_g, mem_post_g, mem_wq, mem_wk, mem_wv, mem_wo, mlp_pre_g, mlp_post_g, mlp_w_up, mlp_w_down):
    b, s, d = x.shape
    assert (b, s, d) == (1, SEQ, D_MODEL)
    h = x.reshape(s, d)
    mem2 = mem.reshape(MEM_LEN, d)
    rope_tabs = _rope_tables(s)
    row = lambda v: v.reshape(1, -1)

    for layer in range(DEPTH):
        if layer % 2 == 0:
            e = layer // 2
            sw_q0 = 3 * NA_W
            proj = _norm_matmul(
                h, row(mix_pre_g[layer]), even_w_in[e].astype(_BF16),
                _col_scale(EVEN_IN, [(0, NA_W), (sw_q0, sw_q0 + SW_QW)]),
                rope_tabs, (sw_q0, sw_q0 + SW_QW + SW_KVW), tm=1024, tn=512, name=f"in_proj{layer}")
            out_a = _na_attention(proj, proj[:, 2 * NA_W:3 * NA_W].T, _na_bias_table(na_rpb[e], s // GRID_W),
                                  blocks_per_step=4, name=f"na{layer}")
            out_b = _swa_attention(proj, proj[:, EVEN_IN - SW_KVW:].T, sw_sinks[e] * _LOG2E,
                                   tq=256, name=f"swa{layer}")
            w_out = even_w_out[e].astype(_BF16)
            h = _matmul_norm_res([out_a, out_b], [w_out[:NA_W], w_out[NA_W:]], h,
                                 row(mix_post_g[layer]), tm=512, name=f"out_proj{layer}")
        else:
            o = layer // 2
            lambda_init = 0.8 - 0.6 * math.exp(-0.3 * layer)
            proj = _norm_matmul(
                h, row(mix_pre_g[layer]), odd_w_in[o].astype(_BF16),
                _col_scale(ODD_IN, [(0, DIFF_W)]),
                rope_tabs, (0, 2 * DIFF_W), tm=1024, tn=512, name=f"in_proj{layer}")
            y = _diff_attention(proj, proj[:, 2 * DIFF_W:].T, row(diff_lam_q1[o]), row(diff_lam_k1[o]), row(diff_lam_q2[o]),
                                row(diff_lam_k2[o]), row(diff_subln_g[o]), lambda_init,
                                tq=1024, tk=512, name=f"diff{layer}")
            h = _matmul_norm_res([y], [odd_w_out[o].astype(_BF16)], h, row(mix_post_g[layer]),
                                 tm=512, name=f"out_proj{layer}")

        wkv = jnp.concatenate([mem_wk[layer], mem_wv[layer]], axis=1).astype(_BF16)
        kv = _norm_matmul(mem2, row(mem_norm_g[layer]), wkv, jnp.ones((1, 2 * MEM_W), _F32),
                          tm=MEM_LEN, tn=512, name=f"mem_kv{layer}")
        h = _mem_attn(h, row(mem_pre_g[layer]), mem_wq[layer].astype(_BF16), kv,
                      mem_wo[layer].astype(_BF16), row(mem_post_g[layer]), tm=512, name=f"mem_attn{layer}")
        h = _mlp(h, row(mlp_pre_g[layer]), mlp_w_up[layer].astype(_BF16), mlp_w_down[layer].astype(_BF16),
                 row(mlp_post_g[layer]), tm=512, tf=1024, name=f"mlp{layer}")
    return h.reshape(b, s, d)
```

```python
import functools
import math

import jax
import jax.numpy as jnp
import numpy as np
from jax import lax
from jax.experimental import pallas as pl
from jax.experimental.pallas import tpu as pltpu

D_MODEL = 2048
SEQ = 8192
DEPTH = 2
GRID_W = 64
HEAD_DIM = 128
ROPE_THETA = 500000.0
ROPE_DIM = HEAD_DIM // 4
NORM_EPS = 1e-6
NA_HEADS = 8
NA_KH = 8
NA_KW = 16
SW_Q_HEADS = 8
SW_KV_HEADS = 2
SW_WINDOW = 128
DIFF_HEADS = 8
MEM_LEN = 256
MEM_HEADS = 4
D_FF = 4 * D_MODEL
NA_W = NA_HEADS * HEAD_DIM
SW_QW = SW_Q_HEADS * HEAD_DIM
SW_KVW = SW_KV_HEADS * HEAD_DIM
EVEN_IN = 3 * NA_W + SW_QW + 2 * SW_KVW
DIFF_W = DIFF_HEADS * 2 * HEAD_DIM
ODD_IN = 3 * DIFF_W
MEM_W = MEM_HEADS * HEAD_DIM

V7X_LANES = 128
V7X_VMEM_BYTES = 64 * 1024 * 1024

_F32 = jnp.float32
_BF16 = jnp.bfloat16
_LOG2E = 1.4426950408889634
_QK_SCALE = HEAD_DIM ** -0.5 * _LOG2E
_MASKED = -1e30
_NT = (((1,), (1,)), ((), ()))


def _nbytes(shape, dtype):
    return math.prod(shape) * jnp.dtype(dtype).itemsize


def _vmem_limit(buffers):
    need = sum(_nbytes(s, d) * n for s, d, n in buffers)
    return min(V7X_VMEM_BYTES - (4 << 20), need + (8 << 20))


def _rms(x, g):
    ms = jnp.mean(x * x, axis=-1, keepdims=True)
    return x * lax.rsqrt(ms + NORM_EPS) * g


def _rope_rot(a, c, s1, s2):
    return (a * c + pltpu.roll(a, V7X_LANES - ROPE_DIM // 2, 1) * s1
            + pltpu.roll(a, ROPE_DIM // 2, 1) * s2)


def _norm_matmul_kernel(x_ref, g_ref, w_ref, cs_ref, *rest, nchunk, rope_lo, rope_hi):
    if rope_lo is None:
        o_ref, hn_ref = rest
    else:
        c_ref, s1_ref, s2_ref, o_ref, hn_ref = rest
    j = pl.program_id(1)

    @pl.when(j == 0)
    def _():
        hn_ref[...] = _rms(x_ref[...], g_ref[...]).astype(_BF16)

    acc = jnp.dot(hn_ref[...], w_ref[...], preferred_element_type=_F32) * cs_ref[...]
    for c in range(nchunk):
        sl = slice(c * V7X_LANES, (c + 1) * V7X_LANES)
        a = acc[:, sl]
        if rope_lo is None:
            o_ref[:, sl] = a.astype(_BF16)
        else:
            gc = j * nchunk + c
            is_rope = jnp.logical_and(gc >= rope_lo, gc < rope_hi)

            @pl.when(is_rope)
            def _():
                o_ref[:, sl] = _rope_rot(a, c_ref[...], s1_ref[...], s2_ref[...]).astype(_BF16)

            @pl.when(jnp.logical_not(is_rope))
            def _():
                o_ref[:, sl] = a.astype(_BF16)


def _norm_matmul(x, g, w, colscale, rope_tabs=None, rope_cols=None, *, tm, tn, name):
    s, d = x.shape
    n = w.shape[1]
    nchunk = tn // V7X_LANES
    in_specs = [
        pl.BlockSpec((tm, d), lambda i, j: (i, 0)),
        pl.BlockSpec((1, d), lambda i, j: (0, 0)),
        pl.BlockSpec((d, tn), lambda i, j: (0, j)),
        pl.BlockSpec((1, tn), lambda i, j: (0, j)),
    ]
    args = [x, g, w, colscale]
    rope_lo = rope_hi = None
    if rope_cols is not None:
        rope_lo, rope_hi = rope_cols[0] // V7X_LANES, rope_cols[1] // V7X_LANES
        in_specs += [pl.BlockSpec((tm, V7X_LANES), lambda i, j: (i, 0))] * 3
        args += list(rope_tabs)
    bufs = [((tm, d), _F32, 3), ((d, tn), _BF16, 2), ((tm, tn), _F32, 2), ((tm, d), _BF16, 1)]
    return pl.pallas_call(
        functools.partial(_norm_matmul_kernel, nchunk=nchunk, rope_lo=rope_lo, rope_hi=rope_hi),
        out_shape=jax.ShapeDtypeStruct((s, n), _BF16),
        grid=(s // tm, n // tn),
        in_specs=in_specs,
        out_specs=pl.BlockSpec((tm, tn), lambda i, j: (i, j)),
        scratch_shapes=[pltpu.VMEM((tm, d), _BF16)],
        compiler_params=pltpu.CompilerParams(
            dimension_semantics=("parallel", "arbitrary"),
            vmem_limit_bytes=_vmem_limit(bufs)),
        name=name,
    )(*args)


def _matmul_norm_res_kernel(*refs, nparts):
    ys = refs[:nparts]
    ws = refs[nparts:2 * nparts]
    h_ref, g_ref, o_ref = refs[2 * nparts:]
    acc = jnp.dot(ys[0][...], ws[0][...], preferred_element_type=_F32)
    for y_ref, w_ref in zip(ys[1:], ws[1:]):
        acc = acc + jnp.dot(y_ref[...], w_ref[...], preferred_element_type=_F32)
    o_ref[...] = h_ref[...] + _rms(acc, g_ref[...])


def _matmul_norm_res(ys, ws, h, g, *, tm, name):
    s, d = h.shape
    nparts = len(ys)
    in_specs = [pl.BlockSpec((tm, y.shape[1]), lambda i: (i, 0)) for y in ys]
    in_specs += [pl.BlockSpec(w.shape, lambda i: (0, 0)) for w in ws]
    in_specs += [pl.BlockSpec((tm, d), lambda i: (i, 0)), pl.BlockSpec((1, d), lambda i: (0, 0))]
    bufs = [((tm, y.shape[1]), _BF16, 2) for y in ys] + [(w.shape, _BF16, 2) for w in ws]
    bufs += [((tm, d), _F32, 6)]
    return pl.pallas_call(
        functools.partial(_matmul_norm_res_kernel, nparts=nparts),
        out_shape=jax.ShapeDtypeStruct((s, d), _F32),
        grid=(s // tm,),
        in_specs=in_specs,
        out_specs=pl.BlockSpec((tm, d), lambda i: (i, 0)),
        compiler_params=pltpu.CompilerParams(
            dimension_semantics=("parallel",),
            vmem_limit_bytes=_vmem_limit(bufs)),
        name=name,
    )(*ys, *ws, h, g)


def _mem_attn_kernel(h_ref, gpre_ref, wq_ref, kv_ref, wo_ref, gpost_ref, o_ref):
    x = h_ref[...]
    hn = _rms(x, gpre_ref[...]).astype(_BF16)
    q = (jnp.dot(hn, wq_ref[...], preferred_element_type=_F32) * _QK_SCALE).astype(_BF16)
    outs = []
    for hd in range(MEM_HEADS):
        sl = slice(hd * HEAD_DIM, (hd + 1) * HEAD_DIM)
        k = kv_ref[:, sl]
        v = kv_ref[:, MEM_W + hd * HEAD_DIM:MEM_W + (hd + 1) * HEAD_DIM]
        sc = lax.dot_general(q[:, sl], k, _NT, preferred_element_type=_F32)
        m = jnp.max(sc, axis=-1, keepdims=True)
        p = jnp.exp2(sc - m)
        l = jnp.sum(p, axis=-1, keepdims=True)
        o = jnp.dot(p.astype(_BF16), v, preferred_element_type=_F32) / l
        outs.append(o.astype(_BF16))
    o_all = jnp.concatenate(outs, axis=-1)
    c = jnp.dot(o_all, wo_ref[...], preferred_element_type=_F32)
    o_ref[...] = x + _rms(c, gpost_ref[...])


def _mem_attn(h, gpre, wq, kv, wo, gpost, *, tm, name):
    s, d = h.shape
    row = lambda i: (i, 0)
    fixed = lambda i: (0, 0)
    bufs = [((tm, d), _F32, 6), ((tm, d), _BF16, 1), (wq.shape, _BF16, 2), (wo.shape, _BF16, 2),
            (kv.shape, _BF16, 2), ((tm, MEM_W), _F32, 4)]
    return pl.pallas_call(
        _mem_attn_kernel,
        out_shape=jax.ShapeDtypeStruct((s, d), _F32),
        grid=(s // tm,),
        in_specs=[pl.BlockSpec((tm, d), row), pl.BlockSpec((1, d), fixed),
                  pl.BlockSpec(wq.shape, fixed), pl.BlockSpec(kv.shape, fixed),
                  pl.BlockSpec(wo.shape, fixed), pl.BlockSpec((1, d), fixed)],
        out_specs=pl.BlockSpec((tm, d), row),
        compiler_params=pltpu.CompilerParams(
            dimension_semantics=("parallel",),
            vmem_limit_bytes=_vmem_limit(bufs)),
        name=name,
    )(h, gpre, wq, kv, wo, gpost)


def _mlp_kernel(h_ref, gpre_ref, wup_ref, wdn_ref, gpost_ref, o_ref, hn_ref, acc_ref):
    c = pl.program_id(1)

    @pl.when(c == 0)
    def _():
        hn_ref[...] = _rms(h_ref[...], gpre_ref[...]).astype(_BF16)
        acc_ref[...] = jnp.zeros(acc_ref.shape, _F32)

    u = jnp.maximum(jnp.dot(hn_ref[...], wup_ref[...], preferred_element_type=_F32), 0.0)
    acc_ref[...] += jnp.dot((u * u).astype(_BF16), wdn_ref[...], preferred_element_type=_F32)

    @pl.when(c == pl.num_programs(1) - 1)
    def _():
        o_ref[...] = h_ref[...] + _rms(acc_ref[...], gpost_ref[...])


def _mlp(h, gpre, wup, wdn, gpost, layer, *, tm, tf, name):
    s, d = h.shape
    f = wup.shape[2]
    bufs = [((tm, d), _F32, 5), ((tm, d), _BF16, 1), ((d, tf), _BF16, 4), ((tm, tf), _F32, 2)]
    return pl.pallas_call(
        _mlp_kernel,
        out_shape=jax.ShapeDtypeStruct((s, d), _F32),
        grid=(s // tm, f // tf),
        in_specs=[pl.BlockSpec((tm, d), lambda i, c: (i, 0)),
                  pl.BlockSpec((1, d), lambda i, c: (0, 0)),
                  pl.BlockSpec((None, d, tf), lambda i, c: (layer, 0, c)),
                  pl.BlockSpec((None, tf, d), lambda i, c: (layer, c, 0)),
                  pl.BlockSpec((1, d), lambda i, c: (0, 0))],
        out_specs=pl.BlockSpec((tm, d), lambda i, c: (i, 0)),
        scratch_shapes=[pltpu.VMEM((tm, d), _BF16), pltpu.VMEM((tm, d), _F32)],
        compiler_params=pltpu.CompilerParams(
            dimension_semantics=("parallel", "arbitrary"),
            vmem_limit_bytes=_vmem_limit(bufs)),
        name=name,
    )(h, gpre, wup, wdn, gpost)


NA_QROWS = 4
NA_BAND = NA_QROWS + NA_KH


def _na_kernel(q_ref, k_ref, vt_ref, b_ref, o_ref, *, blocks_per_step, n_rows):
    step = pl.program_id(1)
    tq = NA_QROWS * GRID_W
    band = NA_BAND * GRID_W
    n_blocks = n_rows // NA_QROWS
    scs, vts = [], []
    for u in range(blocks_per_step):
        b = step * blocks_per_step + u
        variant = jnp.where(b == 0, 0, jnp.where(b == n_blocks - 1, 2, 1))
        row0 = jnp.clip(b * NA_QROWS - NA_KH // 2, 0, n_rows - NA_BAND)
        k0 = pl.multiple_of(row0 * GRID_W, NA_QROWS * GRID_W)
        sc = lax.dot_general(k_ref[pl.ds(k0, band), :], q_ref[u * tq:(u + 1) * tq, :], _NT,
                             preferred_element_type=_F32)
        scs.append(sc + b_ref[variant, 0])
        vts.append(vt_ref[:, pl.ds(k0, band)])
    ps, ls = [], []
    for sc in scs:
        m = jnp.max(sc, axis=0, keepdims=True)
        p = jnp.exp2(sc - m)
        ls.append(jnp.sum(p, axis=0, keepdims=True))
        ps.append(p.astype(_BF16))
    for u in range(blocks_per_step):
        ot = jnp.dot(vts[u], ps[u], preferred_element_type=_F32) / ls[u]
        o_ref[u * tq:(u + 1) * tq, :] = ot.T.astype(_BF16)


def _na_bias_table(rpb, n_rows):
    col = np.arange(GRID_W)
    cs = np.clip(col - NA_KW // 2, 0, GRID_W - NA_KW)
    kc = np.arange(GRID_W)
    col_ok = (kc[:, None] >= cs[None, :]) & (kc[:, None] < cs[None, :] + NA_KW)
    coff = kc[:, None] - col[None, :] + (NA_KW - 1)
    n_ro, n_co = 2 * NA_KH - 1, 2 * NA_KW - 1
    onehot = ((np.arange(n_co)[:, None, None] == coff[None]) & col_ok[None]).astype(np.float32)
    cols = jnp.einsum("hrc,ckj->hrkj", rpb * _LOG2E, onehot, precision=lax.Precision.HIGHEST)
    cols = jnp.where(col_ok[None, None], cols, _MASKED)
    cols = jnp.pad(cols, ((0, 0), (NA_BAND, NA_BAND), (0, 0), (0, 0)), constant_values=_MASKED)
    tabs = []
    for i0 in (0, 2 * NA_QROWS, n_rows - NA_QROWS):
        row0 = int(np.clip(i0 - NA_KH // 2, 0, n_rows - NA_BAND))
        qi = i0 + np.arange(NA_QROWS)
        rs = np.clip(qi - NA_KH // 2, 0, n_rows - NA_KH)
        krow = row0 + np.arange(NA_BAND)
        row_ok = (krow[:, None] >= rs[None, :]) & (krow[:, None] < rs[None, :] + NA_KH)
        parts = [cols[:, NA_BAND + row0 - int(q) + NA_KH - 1:][:, :NA_BAND] for q in qi]
        t = jnp.stack(parts, axis=2)
        t = jnp.where(row_ok[None, :, :, None, None], t, _MASKED)
        t = jnp.transpose(t, (0, 1, 3, 2, 4))
        tabs.append(t.reshape(NA_HEADS, NA_BAND * GRID_W, NA_QROWS * GRID_W))
    return jnp.stack(tabs).astype(_F32)


def _na_attention(proj, vt, bias_tab, *, blocks_per_step, name):
    s = proj.shape[0]
    n_rows = s // GRID_W
    tq = blocks_per_step * NA_QROWS * GRID_W
    kcol = NA_W // HEAD_DIM
    bshape = (3, 1, NA_BAND * GRID_W, NA_QROWS * GRID_W)
    bufs = [((s, HEAD_DIM), _BF16, 4), (bshape, _F32, 2), ((tq, HEAD_DIM), _BF16, 4),
            ((NA_BAND * GRID_W, NA_QROWS * GRID_W), _F32, 3 * blocks_per_step)]
    return pl.pallas_call(
        functools.partial(_na_kernel, blocks_per_step=blocks_per_step, n_rows=n_rows),
        out_shape=jax.ShapeDtypeStruct((s, NA_W), _BF16),
        grid=(NA_HEADS, n_rows // (NA_QROWS * blocks_per_step)),
        in_specs=[pl.BlockSpec((tq, HEAD_DIM), lambda h, i: (i, h)),
                  pl.BlockSpec((s, HEAD_DIM), lambda h, i: (0, kcol + h)),
                  pl.BlockSpec((HEAD_DIM, s), lambda h, i: (h, 0)),
                  pl.BlockSpec(bshape, lambda h, i: (0, h, 0, 0))],
        out_specs=pl.BlockSpec((tq, HEAD_DIM), lambda h, i: (i, h)),
        compiler_params=pltpu.CompilerParams(
            dimension_semantics=("parallel", "arbitrary"),
            vmem_limit_bytes=_vmem_limit(bufs)),
        name=name,
    )(proj, proj, vt, bias_tab)


def _swa_kernel(sink_ref, q_ref, k_ref, vt_ref, b_ref, o_ref, *, tq, band, seq):
    n = pl.program_id(0)
    group = SW_Q_HEADS // SW_KV_HEADS
    start = pl.multiple_of(jnp.clip(n * tq - SW_WINDOW, 0, seq - band), SW_WINDOW)
    bias = b_ref[0]
    scs = []
    for kvh in range(SW_KV_HEADS):
        kb = k_ref[pl.ds(start, band), kvh * HEAD_DIM:(kvh + 1) * HEAD_DIM]
        q = jnp.concatenate([q_ref[:, (kvh * group + g) * HEAD_DIM:(kvh * group + g + 1) * HEAD_DIM]
                             for g in range(group)], axis=0)
        scs.append(lax.dot_general(kb, q, _NT, preferred_element_type=_F32))
    ps, denoms = [], []
    for kvh in range(SW_KV_HEADS):
        pg, dg = [], []
        for g in range(group):
            sink = sink_ref[kvh * group + g]
            sc = scs[kvh][:, g * tq:(g + 1) * tq] + bias
            m = jnp.maximum(jnp.max(sc, axis=0, keepdims=True), sink)
            p = jnp.exp2(sc - m)
            dg.append(jnp.sum(p, axis=0, keepdims=True) + jnp.exp2(sink - m))
            pg.append(p.astype(_BF16))
        ps.append(jnp.concatenate(pg, axis=1))
        denoms.append(jnp.concatenate(dg, axis=1))
    for kvh in range(SW_KV_HEADS):
        vt = vt_ref[kvh * HEAD_DIM:(kvh + 1) * HEAD_DIM, pl.ds(start, band)]
        ot = jnp.dot(vt, ps[kvh], preferred_element_type=_F32) / denoms[kvh]
        for g in range(group):
            hd = kvh * group + g
            o_ref[:, hd * HEAD_DIM:(hd + 1) * HEAD_DIM] = ot[:, g * tq:(g + 1) * tq].T.astype(_BF16)


def _swa_bias_table(tq, band):
    kr = np.arange(band)[:, None]
    qc = np.arange(tq)[None, :]
    tabs = [np.where(np.abs(qc - (rel + kr)) <= SW_WINDOW, 0.0, _MASKED)
            for rel in (0, -SW_WINDOW, tq - band)]
    return np.stack(tabs).astype(np.float32)


def _swa_attention(proj, vt, sinks_log2, *, tq, name):
    s = proj.shape[0]
    band = tq + 2 * SW_WINDOW
    n_blocks = s // tq
    qcol = 3 * NA_W // SW_QW
    kcol = (3 * NA_W + SW_QW) // SW_KVW
    rows = SW_Q_HEADS // SW_KV_HEADS * tq
    bufs = [((s, SW_KVW), _BF16, 4), ((tq, SW_QW), _BF16, 4), ((band, rows), _F32, 8)]
    variant = lambda n: jnp.where(n == 0, 0, jnp.where(n == n_blocks - 1, 2, 1))
    return pl.pallas_call(
        functools.partial(_swa_kernel, tq=tq, band=band, seq=s),
        out_shape=jax.ShapeDtypeStruct((s, SW_QW), _BF16),
        grid=(n_blocks,),
        in_specs=[pl.BlockSpec(memory_space=pltpu.SMEM),
                  pl.BlockSpec((tq, SW_QW), lambda n: (n, qcol)),
                  pl.BlockSpec((s, SW_KVW), lambda n: (0, kcol)),
                  pl.BlockSpec((SW_KVW, s), lambda n: (0, 0)),
                  pl.BlockSpec((1, band, tq), lambda n: (variant(n), 0, 0))],
        out_specs=pl.BlockSpec((tq, SW_QW), lambda n: (n, 0)),
        compiler_params=pltpu.CompilerParams(
            dimension_semantics=("arbitrary",),
            vmem_limit_bytes=_vmem_limit(bufs)),
        name=name,
    )(sinks_log2, proj, proj, vt, _swa_bias_table(tq, band))


def _diff_kernel(q_ref, k_ref, vt_ref, lq1_ref, lk1_ref, lq2_ref, lk2_ref, g_ref, o_ref,
                 m_ref, l_ref, acc_ref, s_ref, *, tk, seq, lambda_init):
    m_ref[...] = jnp.full(m_ref.shape, -jnp.inf, _F32)
    l_ref[...] = jnp.zeros(l_ref.shape, _F32)
    acc_ref[...] = jnp.zeros(acc_ref.shape, _F32)
    nblk = seq // tk

    def scores(kb, slot):
        k0 = pl.multiple_of(kb * tk, tk)
        for t in range(2):
            sl = slice(t * HEAD_DIM, (t + 1) * HEAD_DIM)
            s_ref[slot + t] = lax.dot_general(k_ref[pl.ds(k0, tk), sl], q_ref[:, sl], _NT,
                                              preferred_element_type=_F32)

    def update(kb, slot):
        k0 = pl.multiple_of(kb * tk, tk)
        vt = vt_ref[:, pl.ds(k0, tk)]
        for t in range(2):
            sc = s_ref[slot + t]
            m_prev = m_ref[t]
            m_new = jnp.maximum(m_prev, jnp.max(sc, axis=0, keepdims=True))
            alpha = jnp.exp2(m_prev - m_new)
            p = jnp.exp2(sc - m_new)
            l_ref[t] = alpha * l_ref[t] + jnp.sum(p, axis=0, keepdims=True)
            acc_ref[t] = alpha * acc_ref[t] + jnp.dot(vt, p.astype(_BF16), preferred_element_type=_F32)
            m_ref[t] = m_new

    scores(0, 0)

    def pair(i, carry):
        scores(2 * i + 1, 2)
        update(2 * i, 0)
        scores(2 * i + 2, 0)
        update(2 * i + 1, 2)
        return carry

    lax.fori_loop(0, nblk // 2 - 1, pair, 0)
    scores(nblk - 1, 2)
    update(nblk - 2, 0)
    update(nblk - 1, 2)

    lam = (jnp.exp(jnp.sum(lq1_ref[...] * lk1_ref[...], axis=-1, keepdims=True))
           - jnp.exp(jnp.sum(lq2_ref[...] * lk2_ref[...], axis=-1, keepdims=True)) + lambda_init)
    ot = acc_ref[0] / l_ref[0] - lam * (acc_ref[1] / l_ref[1])
    o_ref[...] = (_rms(ot.T, g_ref[...]) * (1.0 - lambda_init)).astype(_BF16)


def _diff_attention(proj, vt, lq1, lk1, lq2, lk2, subln_g, lambda_init, *, tq, tk, name):
    s = proj.shape[0]
    hw = 2 * HEAD_DIM
    kcol = DIFF_W // hw
    vec = pl.BlockSpec((1, HEAD_DIM), lambda h, qb: (0, 0))
    bufs = [((s, hw), _BF16, 4), ((tq, hw), _BF16, 4), ((tk, tq), _F32, 6),
            ((2, hw, tq), _F32, 2)]
    return pl.pallas_call(
        functools.partial(_diff_kernel, tk=tk, seq=s, lambda_init=lambda_init),
        out_shape=jax.ShapeDtypeStruct((s, DIFF_W), _BF16),
        grid=(DIFF_HEADS, s // tq),
        in_specs=[pl.BlockSpec((tq, hw), lambda h, qb: (qb, h)),
                  pl.BlockSpec((s, hw), lambda h, qb: (0, kcol + h)),
                  pl.BlockSpec((hw, s), lambda h, qb: (h, 0)),
                  vec, vec, vec, vec,
                  pl.BlockSpec((1, hw), lambda h, qb: (0, 0))],
        out_specs=pl.BlockSpec((tq, hw), lambda h, qb: (qb, h)),
        scratch_shapes=[pltpu.VMEM((2, 1, tq), _F32), pltpu.VMEM((2, 1, tq), _F32),
                        pltpu.VMEM((2, hw, tq), _F32), pltpu.VMEM((4, tk, tq), _F32)],
        compiler_params=pltpu.CompilerParams(
            dimension_semantics=("parallel", "arbitrary"),
            vmem_limit_bytes=_vmem_limit(bufs)),
        name=name,
    )(proj, proj, vt, lq1, lk1, lq2, lk2, subln_g)


def _rope_tables(seq):
    half = ROPE_DIM // 2
    inv = 1.0 / (ROPE_THETA ** (jnp.arange(0, ROPE_DIM, 2, dtype=_F32) / ROPE_DIM))
    ang = jnp.arange(seq, dtype=_F32)[:, None] * inv[None, :]
    cos, sin = jnp.cos(ang), jnp.sin(ang)
    pad = V7X_LANES - ROPE_DIM
    c = jnp.concatenate([cos, cos, jnp.ones((seq, pad), _F32)], axis=1)
    s1 = jnp.concatenate([-sin, jnp.zeros((seq, pad + half), _F32)], axis=1)
    s2 = jnp.concatenate([jnp.zeros((seq, half), _F32), sin, jnp.zeros((seq, pad), _F32)], axis=1)
    return c, s1, s2


def _col_scale(n, ranges):
    cs = jnp.ones((1, n), _F32)
    for lo, hi in ranges:
        cs = cs.at[:, lo:hi].set(_QK_SCALE)
    return cs


def kernel(x, mem, even_w_in, even_w_out, na_rpb, sw_sinks, odd_w_in, odd_w_out, diff_lam_q1, diff_lam_k1, diff_lam_q2, diff_lam_k2, diff_subln_g, mix_pre_g, mix_post_g, mem_norm_g, mem_pre_g, mem_post_g, mem_wq, mem_wk, mem_wv, mem_wo, mlp_pre_g, mlp_post_g, mlp_w_up, mlp_w_down):
    b, s, d = x.shape
    assert (b, s, d) == (1, SEQ, D_MODEL)
    h = x.reshape(s, d)
    mem2 = mem.reshape(MEM_LEN, d)
    rope_tabs = _rope_tables(s)
    row = lambda v: v.reshape(1, -1)
    w_up, w_down = mlp_w_up.astype(_BF16), mlp_w_down.astype(_BF16)

    for layer in range(DEPTH):
        if layer % 2 == 0:
            e = layer // 2
            sw_q0 = 3 * NA_W
            proj = _norm_matmul(
                h, row(mix_pre_g[layer]), even_w_in[e].astype(_BF16),
                _col_scale(EVEN_IN, [(0, NA_W), (sw_q0, sw_q0 + SW_QW)]),
                rope_tabs, (sw_q0, sw_q0 + SW_QW + SW_KVW), tm=1024, tn=512, name=f"in_proj{layer}")
            out_a = _na_attention(proj, proj[:, 2 * NA_W:3 * NA_W].T, _na_bias_table(na_rpb[e], s // GRID_W),
                                  blocks_per_step=4, name=f"na{layer}")
            out_b = _swa_attention(proj, proj[:, EVEN_IN - SW_KVW:].T, sw_sinks[e] * _LOG2E,
                                   tq=256, name=f"swa{layer}")
            w_out = even_w_out[e].astype(_BF16)
            h = _matmul_norm_res([out_a, out_b], [w_out[:NA_W], w_out[NA_W:]], h,
                                 row(mix_post_g[layer]), tm=512, name=f"out_proj{layer}")
        else:
            o = layer // 2
            lambda_init = 0.8 - 0.6 * math.exp(-0.3 * layer)
            proj = _norm_matmul(
                h, row(mix_pre_g[layer]), odd_w_in[o].astype(_BF16),
                _col_scale(ODD_IN, [(0, DIFF_W)]),
                rope_tabs, (0, 2 * DIFF_W), tm=1024, tn=512, name=f"in_proj{layer}")
            y = _diff_attention(proj, proj[:, 2 * DIFF_W:].T, row(diff_lam_q1[o]), row(diff_lam_k1[o]), row(diff_lam_q2[o]),
                                row(diff_lam_k2[o]), row(diff_subln_g[o]), lambda_init,
                                tq=1024, tk=512, name=f"diff{layer}")
            h = _matmul_norm_res([y], [odd_w_out[o].astype(_BF16)], h, row(mix_post_g[layer]),
                                 tm=512, name=f"out_proj{layer}")

        wkv = jnp.concatenate([mem_wk[layer], mem_wv[layer]], axis=1).astype(_BF16)
        kv = _norm_matmul(mem2, row(mem_norm_g[layer]), wkv, jnp.ones((1, 2 * MEM_W), _F32),
                          tm=MEM_LEN, tn=512, name=f"mem_kv{layer}")
        h = _mem_attn(h, row(mem_pre_g[layer]), mem_wq[layer].astype(_BF16), kv,
                      mem_wo[layer].astype(_BF16), row(mem_post_g[layer]), tm=512, name=f"mem_attn{layer}")
        h = _mlp(h, row(mlp_pre_g[layer]), w_up, w_down, row(mlp_post_g[layer]), layer,
                 tm=512, tf=1024, name=f"mlp{layer}")
    return h.reshape(b, s, d)
```

```python
import functools
import math

import jax
import jax.numpy as jnp
import numpy as np
from jax import lax
from jax.experimental import pallas as pl
from jax.experimental.pallas import tpu as pltpu

D_MODEL = 2048
SEQ = 8192
DEPTH = 2
GRID_W = 64
HEAD_DIM = 128
ROPE_THETA = 500000.0
ROPE_DIM = HEAD_DIM // 4
NORM_EPS = 1e-6
NA_HEADS = 8
NA_KH = 8
NA_KW = 16
SW_Q_HEADS = 8
SW_KV_HEADS = 2
SW_WINDOW = 128
DIFF_HEADS = 8
MEM_LEN = 256
MEM_HEADS = 4
D_FF = 4 * D_MODEL
NA_W = NA_HEADS * HEAD_DIM
SW_QW = SW_Q_HEADS * HEAD_DIM
SW_KVW = SW_KV_HEADS * HEAD_DIM
EVEN_IN = 3 * NA_W + SW_QW + 2 * SW_KVW
DIFF_W = DIFF_HEADS * 2 * HEAD_DIM
ODD_IN = 3 * DIFF_W
MEM_W = MEM_HEADS * HEAD_DIM

V7X_LANES = 128
V7X_VMEM_BYTES = 64 * 1024 * 1024

_F32 = jnp.float32
_BF16 = jnp.bfloat16
_LOG2E = 1.4426950408889634
_QK_SCALE = HEAD_DIM ** -0.5 * _LOG2E
_MASKED = -1e30
_NT = (((1,), (1,)), ((), ()))


def _nbytes(shape, dtype):
    return math.prod(shape) * jnp.dtype(dtype).itemsize


def _vmem_limit(buffers):
    need = sum(_nbytes(s, d) * n for s, d, n in buffers)
    return min(V7X_VMEM_BYTES - (4 << 20), need + (8 << 20))


def _rms(x, g):
    ms = jnp.mean(x * x, axis=-1, keepdims=True)
    return x * lax.rsqrt(ms + NORM_EPS) * g


def _rope_rot(a, c, s1, s2):
    return (a * c + pltpu.roll(a, V7X_LANES - ROPE_DIM // 2, 1) * s1
            + pltpu.roll(a, ROPE_DIM // 2, 1) * s2)


def _norm_matmul_kernel(x_ref, g_ref, w_ref, cs_ref, *rest, nchunk, rope_lo, rope_hi):
    if rope_lo is None:
        o_ref, hn_ref = rest
    else:
        c_ref, s1_ref, s2_ref, o_ref, hn_ref = rest
    j = pl.program_id(1)

    @pl.when(j == 0)
    def _():
        hn_ref[...] = _rms(x_ref[...], g_ref[...]).astype(_BF16)

    acc = jnp.dot(hn_ref[...], w_ref[...], preferred_element_type=_F32) * cs_ref[...]
    for c in range(nchunk):
        sl = slice(c * V7X_LANES, (c + 1) * V7X_LANES)
        a = acc[:, sl]
        if rope_lo is None:
            o_ref[:, sl] = a.astype(_BF16)
        else:
            gc = j * nchunk + c
            is_rope = jnp.logical_and(gc >= rope_lo, gc < rope_hi)

            @pl.when(is_rope)
            def _():
                o_ref[:, sl] = _rope_rot(a, c_ref[...], s1_ref[...], s2_ref[...]).astype(_BF16)

            @pl.when(jnp.logical_not(is_rope))
            def _():
                o_ref[:, sl] = a.astype(_BF16)


def _norm_matmul(x, g, w, colscale, rope_tabs=None, rope_cols=None, *, tm, tn, name):
    s, d = x.shape
    n = w.shape[1]
    nchunk = tn // V7X_LANES
    in_specs = [
        pl.BlockSpec((tm, d), lambda i, j: (i, 0)),
        pl.BlockSpec((1, d), lambda i, j: (0, 0)),
        pl.BlockSpec((d, tn), lambda i, j: (0, j)),
        pl.BlockSpec((1, tn), lambda i, j: (0, j)),
    ]
    args = [x, g, w, colscale]
    rope_lo = rope_hi = None
    if rope_cols is not None:
        rope_lo, rope_hi = rope_cols[0] // V7X_LANES, rope_cols[1] // V7X_LANES
        in_specs += [pl.BlockSpec((tm, V7X_LANES), lambda i, j: (i, 0))] * 3
        args += list(rope_tabs)
    bufs = [((tm, d), _F32, 3), ((d, tn), _BF16, 2), ((tm, tn), _F32, 2), ((tm, d), _BF16, 1)]
    return pl.pallas_call(
        functools.partial(_norm_matmul_kernel, nchunk=nchunk, rope_lo=rope_lo, rope_hi=rope_hi),
        out_shape=jax.ShapeDtypeStruct((s, n), _BF16),
        grid=(s // tm, n // tn),
        in_specs=in_specs,
        out_specs=pl.BlockSpec((tm, tn), lambda i, j: (i, j)),
        scratch_shapes=[pltpu.VMEM((tm, d), _BF16)],
        compiler_params=pltpu.CompilerParams(
            dimension_semantics=("parallel", "arbitrary"),
            vmem_limit_bytes=_vmem_limit(bufs)),
        name=name,
    )(*args)


def _post_mix_kernel(*refs, nparts):
    ys = refs[:nparts]
    wout_ref, h_ref, gmix_ref, gpre_ref, wq_ref, kv_ref, wo_ref, gpost_ref, o_ref = refs[nparts:]
    acc, k0 = None, 0
    for y_ref in ys:
        kk = y_ref.shape[1]
        part = jnp.dot(y_ref[...], wout_ref[k0:k0 + kk, :], preferred_element_type=_F32)
        acc = part if acc is None else acc + part
        k0 += kk
    x = h_ref[...] + _rms(acc, gmix_ref[...])
    hn = _rms(x, gpre_ref[...]).astype(_BF16)
    q = (jnp.dot(hn, wq_ref[...], preferred_element_type=_F32) * _QK_SCALE).astype(_BF16)
    outs = []
    for hd in range(MEM_HEADS):
        sl = slice(hd * HEAD_DIM, (hd + 1) * HEAD_DIM)
        k = kv_ref[:, sl]
        v = kv_ref[:, MEM_W + hd * HEAD_DIM:MEM_W + (hd + 1) * HEAD_DIM]
        sc = lax.dot_general(q[:, sl], k, _NT, preferred_element_type=_F32)
        m = jnp.max(sc, axis=-1, keepdims=True)
        p = jnp.exp2(sc - m)
        l = jnp.sum(p, axis=-1, keepdims=True)
        o = jnp.dot(p.astype(_BF16), v, preferred_element_type=_F32) / l
        outs.append(o.astype(_BF16))
    o_all = jnp.concatenate(outs, axis=-1)
    c = jnp.dot(o_all, wo_ref[...], preferred_element_type=_F32)
    o_ref[...] = x + _rms(c, gpost_ref[...])


def _post_mix(ys, wout, h, gmix, gpre, wq, kv, wo, gpost, *, tm, name):
    s, d = h.shape
    nparts = len(ys)
    row = lambda i: (i, 0)
    fixed = lambda i: (0, 0)
    whole = lambda a: pl.BlockSpec(a.shape, fixed, pipeline_mode=pl.Buffered(1))
    in_specs = [pl.BlockSpec((tm, y.shape[1]), row) for y in ys]
    in_specs += [whole(wout), pl.BlockSpec((tm, d), row), whole(gmix), whole(gpre), whole(wq), whole(kv),
                 whole(wo), whole(gpost)]
    bufs = [((tm, y.shape[1]), _BF16, 2) for y in ys]
    bufs += [(wout.shape, _BF16, 1), (wq.shape, _BF16, 1), (wo.shape, _BF16, 1), (kv.shape, _BF16, 1),
             ((tm, d), _F32, 8), ((tm, d), _BF16, 1), ((tm, MEM_W), _F32, 4)]
    return pl.pallas_call(
        functools.partial(_post_mix_kernel, nparts=nparts),
        out_shape=jax.ShapeDtypeStruct((s, d), _F32),
        grid=(s // tm,),
        in_specs=in_specs,
        out_specs=pl.BlockSpec((tm, d), row),
        compiler_params=pltpu.CompilerParams(
            dimension_semantics=("parallel",),
            vmem_limit_bytes=_vmem_limit(bufs)),
        name=name,
    )(*ys, wout, h, gmix, gpre, wq, kv, wo, gpost)


def _mlp_kernel(h_ref, gpre_ref, wup_ref, wdn_ref, gpost_ref, o_ref, hn_ref, acc_ref):
    c = pl.program_id(1)

    @pl.when(c == 0)
    def _():
        hn_ref[...] = _rms(h_ref[...], gpre_ref[...]).astype(_BF16)
        acc_ref[...] = jnp.zeros(acc_ref.shape, _F32)

    u = jnp.maximum(jnp.dot(hn_ref[...], wup_ref[...], preferred_element_type=_F32), 0.0)
    acc_ref[...] += jnp.dot((u * u).astype(_BF16), wdn_ref[...], preferred_element_type=_F32)

    @pl.when(c == pl.num_programs(1) - 1)
    def _():
        o_ref[...] = h_ref[...] + _rms(acc_ref[...], gpost_ref[...])


def _mlp(h, gpre, wup, wdn, gpost, layer, *, tm, tf, name):
    s, d = h.shape
    f = wup.shape[2]
    bufs = [((tm, d), _F32, 5), ((tm, d), _BF16, 1), ((d, tf), _BF16, 4), ((tm, tf), _F32, 2)]
    return pl.pallas_call(
        _mlp_kernel,
        out_shape=jax.ShapeDtypeStruct((s, d), _F32),
        grid=(s // tm, f // tf),
        in_specs=[pl.BlockSpec((tm, d), lambda i, c: (i, 0)),
                  pl.BlockSpec((1, d), lambda i, c: (0, 0)),
                  pl.BlockSpec((None, d, tf), lambda i, c: (layer, 0, c)),
                  pl.BlockSpec((None, tf, d), lambda i, c: (layer, c, 0)),
                  pl.BlockSpec((1, d), lambda i, c: (0, 0))],
        out_specs=pl.BlockSpec((tm, d), lambda i, c: (i, 0)),
        scratch_shapes=[pltpu.VMEM((tm, d), _BF16), pltpu.VMEM((tm, d), _F32)],
        compiler_params=pltpu.CompilerParams(
            dimension_semantics=("parallel", "arbitrary"),
            vmem_limit_bytes=_vmem_limit(bufs)),
        name=name,
    )(h, gpre, wup, wdn, gpost)


NA_QROWS = 4
NA_BAND = NA_QROWS + NA_KH


def _na_kernel(q_ref, k_ref, vt_ref, bfirst_ref, bmid_ref, blast_ref, o_ref, *, blocks_per_step, n_rows):
    step = pl.program_id(1)
    tq = NA_QROWS * GRID_W
    band = NA_BAND * GRID_W
    scs, vts = [], []
    for u in range(blocks_per_step):
        b = step * blocks_per_step + u
        row0 = jnp.clip(b * NA_QROWS - NA_KH // 2, 0, n_rows - NA_BAND)
        k0 = pl.multiple_of(row0 * GRID_W, NA_QROWS * GRID_W)
        sc = lax.dot_general(k_ref[pl.ds(k0, band), :], q_ref[u * tq:(u + 1) * tq, :], _NT,
                             preferred_element_type=_F32)
        bias = bmid_ref[0]
        if u == 0:
            bias = jnp.where(step == 0, bfirst_ref[0], bias)
        if u == blocks_per_step - 1:
            bias = jnp.where(step == n_rows // (NA_QROWS * blocks_per_step) - 1, blast_ref[0], bias)
        scs.append(sc + bias)
        vts.append(vt_ref[:, pl.ds(k0, band)])
    ps, ls = [], []
    for sc in scs:
        m = jnp.max(sc, axis=0, keepdims=True)
        p = jnp.exp2(sc - m)
        ls.append(jnp.sum(p, axis=0, keepdims=True))
        ps.append(p.astype(_BF16))
    for u in range(blocks_per_step):
        ot = jnp.dot(vts[u], ps[u], preferred_element_type=_F32) / ls[u]
        o_ref[u * tq:(u + 1) * tq, :] = ot.T.astype(_BF16)


def _na_bias_table(rpb, n_rows):
    col = np.arange(GRID_W)
    cs = np.clip(col - NA_KW // 2, 0, GRID_W - NA_KW)
    kc = np.arange(GRID_W)
    col_ok = (kc[:, None] >= cs[None, :]) & (kc[:, None] < cs[None, :] + NA_KW)
    coff = kc[:, None] - col[None, :] + (NA_KW - 1)
    n_ro, n_co = 2 * NA_KH - 1, 2 * NA_KW - 1
    onehot = ((np.arange(n_co)[:, None, None] == coff[None]) & col_ok[None]).astype(np.float32)
    cols = jnp.einsum("hrc,ckj->hrkj", rpb * _LOG2E, onehot, precision=lax.Precision.HIGHEST)
    cols = jnp.where(col_ok[None, None], cols, _MASKED)
    cols = jnp.pad(cols, ((0, 0), (NA_BAND, NA_BAND), (0, 0), (0, 0)), constant_values=_MASKED)
    cols = cols.reshape(NA_HEADS, (n_ro + 2 * NA_BAND) * GRID_W, GRID_W)
    tabs = []
    for i0 in (0, 2 * NA_QROWS, n_rows - NA_QROWS):
        row0 = int(np.clip(i0 - NA_KH // 2, 0, n_rows - NA_BAND))
        qi = i0 + np.arange(NA_QROWS)
        rs = np.clip(qi - NA_KH // 2, 0, n_rows - NA_KH)
        krow = row0 + np.arange(NA_BAND)
        row_ok = (krow[:, None] >= rs[None, :]) & (krow[:, None] < rs[None, :] + NA_KH)
        parts = []
        for qr, q in enumerate(qi):
            lo = (NA_BAND + row0 - int(q) + NA_KH - 1) * GRID_W
            ok = np.repeat(row_ok[:, qr], GRID_W)
            parts.append(jnp.where(ok[None, :, None], cols[:, lo:lo + NA_BAND * GRID_W], _MASKED))
        tabs.append(jnp.concatenate(parts, axis=2).astype(_F32))
    return tabs


def _na_attention(proj, vt, bias_tabs, *, blocks_per_step, name):
    s = proj.shape[0]
    n_rows = s // GRID_W
    assert blocks_per_step >= 2
    tq = blocks_per_step * NA_QROWS * GRID_W
    kcol = NA_W // HEAD_DIM
    bshape = (1, NA_BAND * GRID_W, NA_QROWS * GRID_W)
    bspecs = [pl.BlockSpec(bshape, lambda h, i: (h, 0, 0)) for _ in bias_tabs]
    bufs = [((s, HEAD_DIM), _BF16, 4), (bshape, _F32, 6), ((tq, HEAD_DIM), _BF16, 4),
            ((NA_BAND * GRID_W, NA_QROWS * GRID_W), _F32, 3 * blocks_per_step)]
    return pl.pallas_call(
        functools.partial(_na_kernel, blocks_per_step=blocks_per_step, n_rows=n_rows),
        out_shape=jax.ShapeDtypeStruct((s, NA_W), _BF16),
        grid=(NA_HEADS, n_rows // (NA_QROWS * blocks_per_step)),
        in_specs=[pl.BlockSpec((tq, HEAD_DIM), lambda h, i: (i, h)),
                  pl.BlockSpec((s, HEAD_DIM), lambda h, i: (0, kcol + h)),
                  pl.BlockSpec((HEAD_DIM, s), lambda h, i: (h, 0)),
                  *bspecs],
        out_specs=pl.BlockSpec((tq, HEAD_DIM), lambda h, i: (i, h)),
        compiler_params=pltpu.CompilerParams(
            dimension_semantics=("parallel", "arbitrary"),
            vmem_limit_bytes=_vmem_limit(bufs)),
        name=name,
    )(proj, proj, vt, *bias_tabs)


def _swa_kernel(sink_ref, q_ref, k_ref, vt_ref, b_ref, o_ref, *, tq, band, seq):
    n = pl.program_id(0)
    group = SW_Q_HEADS // SW_KV_HEADS
    start = pl.multiple_of(jnp.clip(n * tq - SW_WINDOW, 0, seq - band), SW_WINDOW)
    bias = b_ref[0]
    scs = []
    for kvh in range(SW_KV_HEADS):
        kb = k_ref[pl.ds(start, band), kvh * HEAD_DIM:(kvh + 1) * HEAD_DIM]
        q = jnp.concatenate([q_ref[:, (kvh * group + g) * HEAD_DIM:(kvh * group + g + 1) * HEAD_DIM]
                             for g in range(group)], axis=0)
        scs.append(lax.dot_general(kb, q, _NT, preferred_element_type=_F32))
    ps, denoms = [], []
    for kvh in range(SW_KV_HEADS):
        pg, dg = [], []
        for g in range(group):
            sink = sink_ref[kvh * group + g]
            sc = scs[kvh][:, g * tq:(g + 1) * tq] + bias
            m = jnp.maximum(jnp.max(sc, axis=0, keepdims=True), sink)
            p = jnp.exp2(sc - m)
            dg.append(jnp.sum(p, axis=0, keepdims=True) + jnp.exp2(sink - m))
            pg.append(p.astype(_BF16))
        ps.append(jnp.concatenate(pg, axis=1))
        denoms.append(jnp.concatenate(dg, axis=1))
    for kvh in range(SW_KV_HEADS):
        vt = vt_ref[kvh * HEAD_DIM:(kvh + 1) * HEAD_DIM, pl.ds(start, band)]
        ot = jnp.dot(vt, ps[kvh], preferred_element_type=_F32) / denoms[kvh]
        for g in range(group):
            hd = kvh * group + g
            o_ref[:, hd * HEAD_DIM:(hd + 1) * HEAD_DIM] = ot[:, g * tq:(g + 1) * tq].T.astype(_BF16)


def _swa_bias_table(tq, band):
    kr = np.arange(band)[:, None]
    qc = np.arange(tq)[None, :]
    tabs = [np.where(np.abs(qc - (rel + kr)) <= SW_WINDOW, 0.0, _MASKED)
            for rel in (0, -SW_WINDOW, tq - band)]
    return jnp.asarray(np.stack(tabs), _F32)


def _swa_attention(proj, vt, sinks_log2, *, tq, name):
    s = proj.shape[0]
    band = tq + 2 * SW_WINDOW
    n_blocks = s // tq
    qcol = 3 * NA_W // SW_QW
    kcol = (3 * NA_W + SW_QW) // SW_KVW
    rows = SW_Q_HEADS // SW_KV_HEADS * tq
    bufs = [((s, SW_KVW), _BF16, 4), ((tq, SW_QW), _BF16, 4), ((band, rows), _F32, 8)]
    variant = lambda n: jnp.where(n == 0, 0, jnp.where(n == n_blocks - 1, 2, 1))
    return pl.pallas_call(
        functools.partial(_swa_kernel, tq=tq, band=band, seq=s),
        out_shape=jax.ShapeDtypeStruct((s, SW_QW), _BF16),
        grid=(n_blocks,),
        in_specs=[pl.BlockSpec(memory_space=pltpu.SMEM),
                  pl.BlockSpec((tq, SW_QW), lambda n: (n, qcol)),
                  pl.BlockSpec((s, SW_KVW), lambda n: (0, kcol)),
                  pl.BlockSpec((SW_KVW, s), lambda n: (0, 0)),
                  pl.BlockSpec((1, band, tq), lambda n: (variant(n), 0, 0))],
        out_specs=pl.BlockSpec((tq, SW_QW), lambda n: (n, 0)),
        compiler_params=pltpu.CompilerParams(
            dimension_semantics=("arbitrary",),
            vmem_limit_bytes=_vmem_limit(bufs)),
        name=name,
    )(sinks_log2, proj, proj, vt, _swa_bias_table(tq, band))


def _diff_kernel(q_ref, k_ref, vt_ref, lq1_ref, lk1_ref, lq2_ref, lk2_ref, g_ref, o_ref,
                 m_ref, l_ref, acc_ref, s_ref, *, tk, seq, lambda_init):
    m_ref[...] = jnp.full(m_ref.shape, -jnp.inf, _F32)
    l_ref[...] = jnp.zeros(l_ref.shape, _F32)
    acc_ref[...] = jnp.zeros(acc_ref.shape, _F32)
    nblk = seq // tk

    def scores(kb, slot):
        k0 = pl.multiple_of(kb * tk, tk)
        for t in range(2):
            sl = slice(t * HEAD_DIM, (t + 1) * HEAD_DIM)
            s_ref[slot + t] = lax.dot_general(k_ref[pl.ds(k0, tk), sl], q_ref[:, sl], _NT,
                                              preferred_element_type=_F32)

    def update(kb, slot):
        k0 = pl.multiple_of(kb * tk, tk)
        vt = vt_ref[:, pl.ds(k0, tk)]
        for t in range(2):
            sc = s_ref[slot + t]
            m_prev = m_ref[t]
            m_new = jnp.maximum(m_prev, jnp.max(sc, axis=0, keepdims=True))
            alpha = jnp.exp2(m_prev - m_new)
            p = jnp.exp2(sc - m_new)
            l_ref[t] = alpha * l_ref[t] + jnp.sum(p, axis=0, keepdims=True)
            acc_ref[t] = alpha * acc_ref[t] + jnp.dot(vt, p.astype(_BF16), preferred_element_type=_F32)
            m_ref[t] = m_new

    scores(0, 0)

    def pair(i, carry):
        scores(2 * i + 1, 2)
        update(2 * i, 0)
        scores(2 * i + 2, 0)
        update(2 * i + 1, 2)
        return carry

    lax.fori_loop(0, nblk // 2 - 1, pair, 0)
    scores(nblk - 1, 2)
    update(nblk - 2, 0)
    update(nblk - 1, 2)

    lam = (jnp.exp(jnp.sum(lq1_ref[...] * lk1_ref[...], axis=-1, keepdims=True))
           - jnp.exp(jnp.sum(lq2_ref[...] * lk2_ref[...], axis=-1, keepdims=True)) + lambda_init)
    ot = acc_ref[0] / l_ref[0] - lam * (acc_ref[1] / l_ref[1])
    o_ref[...] = (_rms(ot.T, g_ref[...]) * (1.0 - lambda_init)).astype(_BF16)


def _diff_attention(proj, vt, lq1, lk1, lq2, lk2, subln_g, lambda_init, *, tq, tk, name):
    s = proj.shape[0]
    hw = 2 * HEAD_DIM
    kcol = DIFF_W // hw
    vec = pl.BlockSpec((1, HEAD_DIM), lambda h, qb: (0, 0))
    bufs = [((s, hw), _BF16, 4), ((tq, hw), _BF16, 4), ((tk, tq), _F32, 6),
            ((2, hw, tq), _F32, 2)]
    return pl.pallas_call(
        functools.partial(_diff_kernel, tk=tk, seq=s, lambda_init=lambda_init),
        out_shape=jax.ShapeDtypeStruct((s, DIFF_W), _BF16),
        grid=(DIFF_HEADS, s // tq),
        in_specs=[pl.BlockSpec((tq, hw), lambda h, qb: (qb, h)),
                  pl.BlockSpec((s, hw), lambda h, qb: (0, kcol + h)),
                  pl.BlockSpec((hw, s), lambda h, qb: (h, 0)),
                  vec, vec, vec, vec,
                  pl.BlockSpec((1, hw), lambda h, qb: (0, 0))],
        out_specs=pl.BlockSpec((tq, hw), lambda h, qb: (qb, h)),
        scratch_shapes=[pltpu.VMEM((2, 1, tq), _F32), pltpu.VMEM((2, 1, tq), _F32),
                        pltpu.VMEM((2, hw, tq), _F32), pltpu.VMEM((4, tk, tq), _F32)],
        compiler_params=pltpu.CompilerParams(
            dimension_semantics=("parallel", "arbitrary"),
            vmem_limit_bytes=_vmem_limit(bufs)),
        name=name,
    )(proj, proj, vt, lq1, lk1, lq2, lk2, subln_g)


def _rope_tables(seq):
    half = ROPE_DIM // 2
    inv = 1.0 / (ROPE_THETA ** (np.arange(0, ROPE_DIM, 2, dtype=np.float64) / ROPE_DIM))
    ang = np.arange(seq, dtype=np.float64)[:, None] * inv[None, :]
    cos, sin = np.cos(ang), np.sin(ang)
    pad = V7X_LANES - ROPE_DIM
    c = np.concatenate([cos, cos, np.ones((seq, pad))], axis=1)
    s1 = np.concatenate([-sin, np.zeros((seq, pad + half))], axis=1)
    s2 = np.concatenate([np.zeros((seq, half)), sin, np.zeros((seq, pad))], axis=1)
    return tuple(jnp.asarray(t, _F32) for t in (c, s1, s2))


def _col_scale(n, ranges):
    cs = np.ones((1, n), np.float32)
    for lo, hi in ranges:
        cs[:, lo:hi] = _QK_SCALE
    return jnp.asarray(cs)


def kernel(x, mem, even_w_in, even_w_out, na_rpb, sw_sinks, odd_w_in, odd_w_out, diff_lam_q1, diff_lam_k1, diff_lam_q2, diff_lam_k2, diff_subln_g, mix_pre_g, mix_post_g, mem_norm_g, mem_pre_g, mem_post_g, mem_wq, mem_wk, mem_wv, mem_wo, mlp_pre_g, mlp_post_g, mlp_w_up, mlp_w_down):
    b, s, d = x.shape
    assert (b, s, d) == (1, SEQ, D_MODEL)
    h = x.reshape(s, d)
    mem2 = mem.reshape(MEM_LEN, d)
    rope_tabs = _rope_tables(s)
    row = lambda v: v.reshape(1, -1)
    w_up, w_down = mlp_w_up.astype(_BF16), mlp_w_down.astype(_BF16)

    for layer in range(DEPTH):
        if layer % 2 == 0:
            e = layer // 2
            sw_q0 = 3 * NA_W
            proj = _norm_matmul(
                h, row(mix_pre_g[layer]), even_w_in[e].astype(_BF16),
                _col_scale(EVEN_IN, [(0, NA_W), (sw_q0, sw_q0 + SW_QW)]),
                rope_tabs, (sw_q0, sw_q0 + SW_QW + SW_KVW), tm=1024, tn=512, name=f"in_proj{layer}")
            out_a = _na_attention(proj, proj[:, 2 * NA_W:3 * NA_W].T, _na_bias_table(na_rpb[e], s // GRID_W),
                                  blocks_per_step=4, name=f"na{layer}")
            out_b = _swa_attention(proj, proj[:, EVEN_IN - SW_KVW:].T, sw_sinks[e] * _LOG2E,
                                   tq=256, name=f"swa{layer}")
            ys, w_out = [out_a, out_b], even_w_out[e].astype(_BF16)
        else:
            o = layer // 2
            lambda_init = 0.8 - 0.6 * math.exp(-0.3 * layer)
            proj = _norm_matmul(
                h, row(mix_pre_g[layer]), odd_w_in[o].astype(_BF16),
                _col_scale(ODD_IN, [(0, DIFF_W)]),
                rope_tabs, (0, 2 * DIFF_W), tm=1024, tn=512, name=f"in_proj{layer}")
            y = _diff_attention(proj, proj[:, 2 * DIFF_W:].T, row(diff_lam_q1[o]), row(diff_lam_k1[o]), row(diff_lam_q2[o]),
                                row(diff_lam_k2[o]), row(diff_subln_g[o]), lambda_init,
                                tq=1024, tk=512, name=f"diff{layer}")
            ys, w_out = [y], odd_w_out[o].astype(_BF16)

        wkv = jnp.concatenate([mem_wk[layer], mem_wv[layer]], axis=1).astype(_BF16)
        kv = _norm_matmul(mem2, row(mem_norm_g[layer]), wkv, jnp.ones((1, 2 * MEM_W), _F32),
                          tm=MEM_LEN, tn=512, name=f"mem_kv{layer}")
        h = _post_mix(ys, w_out, h, row(mix_post_g[layer]), row(mem_pre_g[layer]),
                      mem_wq[layer].astype(_BF16), kv, mem_wo[layer].astype(_BF16),
                      row(mem_post_g[layer]), tm=512, name=f"post_mix{layer}")
        h = _mlp(h, row(mlp_pre_g[layer]), w_up, w_down, row(mlp_post_g[layer]), layer,
                 tm=512, tf=1024, name=f"mlp{layer}")
    return h.reshape(b, s, d)
```

```python
import functools
import math

import jax
import jax.numpy as jnp
import numpy as np
from jax import lax
from jax.experimental import pallas as pl
from jax.experimental.pallas import tpu as pltpu

D_MODEL = 2048
SEQ = 8192
DEPTH = 2
GRID_W = 64
HEAD_DIM = 128
ROPE_THETA = 500000.0
ROPE_DIM = HEAD_DIM // 4
NORM_EPS = 1e-6
NA_HEADS = 8
NA_KH = 8
NA_KW = 16
SW_Q_HEADS = 8
SW_KV_HEADS = 2
SW_WINDOW = 128
DIFF_HEADS = 8
MEM_LEN = 256
MEM_HEADS = 4
D_FF = 4 * D_MODEL
NA_W = NA_HEADS * HEAD_DIM
SW_QW = SW_Q_HEADS * HEAD_DIM
SW_KVW = SW_KV_HEADS * HEAD_DIM
EVEN_IN = 3 * NA_W + SW_QW + 2 * SW_KVW
DIFF_W = DIFF_HEADS * 2 * HEAD_DIM
ODD_IN = 3 * DIFF_W
MEM_W = MEM_HEADS * HEAD_DIM

V7X_LANES = 128
V7X_VMEM_BYTES = 64 * 1024 * 1024

_F32 = jnp.float32
_BF16 = jnp.bfloat16
_LOG2E = 1.4426950408889634
_QK_SCALE = HEAD_DIM ** -0.5 * _LOG2E
_MASKED = -1e30
_NT = (((1,), (1,)), ((), ()))


def _nbytes(shape, dtype):
    return math.prod(shape) * jnp.dtype(dtype).itemsize


def _vmem_limit(buffers):
    need = sum(_nbytes(s, d) * n for s, d, n in buffers)
    return min(V7X_VMEM_BYTES - (4 << 20), need + (8 << 20))


def _rms(x, g):
    ms = jnp.mean(x * x, axis=-1, keepdims=True)
    return x * lax.rsqrt(ms + NORM_EPS) * g


def _rope_rot(a, c, s1, s2):
    return (a * c + pltpu.roll(a, V7X_LANES - ROPE_DIM // 2, 1) * s1
            + pltpu.roll(a, ROPE_DIM // 2, 1) * s2)


def _norm_matmul_kernel(x_ref, g_ref, w_ref, cs_ref, *rest, nchunk, rope_lo, rope_hi):
    if rope_lo is None:
        o_ref, hn_ref = rest
    else:
        c_ref, s1_ref, s2_ref, o_ref, hn_ref = rest
    j = pl.program_id(1)

    @pl.when(j == 0)
    def _():
        hn_ref[...] = _rms(x_ref[...], g_ref[...]).astype(_BF16)

    acc = jnp.dot(hn_ref[...], w_ref[...].astype(_BF16), preferred_element_type=_F32) * cs_ref[...]
    for c in range(nchunk):
        sl = slice(c * V7X_LANES, (c + 1) * V7X_LANES)
        a = acc[:, sl]
        if rope_lo is None:
            o_ref[:, sl] = a.astype(_BF16)
        else:
            gc = j * nchunk + c
            is_rope = jnp.logical_and(gc >= rope_lo, gc < rope_hi)

            @pl.when(is_rope)
            def _():
                o_ref[:, sl] = _rope_rot(a, c_ref[...], s1_ref[...], s2_ref[...]).astype(_BF16)

            @pl.when(jnp.logical_not(is_rope))
            def _():
                o_ref[:, sl] = a.astype(_BF16)


def _norm_matmul(x, g, w, colscale, rope_tabs=None, rope_cols=None, *, tm, tn, name):
    s, d = x.shape
    n = w.shape[1]
    nchunk = tn // V7X_LANES
    in_specs = [
        pl.BlockSpec((tm, d), lambda i, j: (i, 0)),
        pl.BlockSpec((1, d), lambda i, j: (0, 0)),
        pl.BlockSpec((d, tn), lambda i, j: (0, j)),
        pl.BlockSpec((1, tn), lambda i, j: (0, j)),
    ]
    args = [x, g, w, colscale]
    rope_lo = rope_hi = None
    if rope_cols is not None:
        rope_lo, rope_hi = rope_cols[0] // V7X_LANES, rope_cols[1] // V7X_LANES
        in_specs += [pl.BlockSpec((tm, V7X_LANES), lambda i, j: (i, 0))] * 3
        args += list(rope_tabs)
    bufs = [((tm, d), _F32, 3), ((d, tn), w.dtype, 2), ((d, tn), _BF16, 1), ((tm, tn), _F32, 2),
            ((tm, d), _BF16, 1)]
    return pl.pallas_call(
        functools.partial(_norm_matmul_kernel, nchunk=nchunk, rope_lo=rope_lo, rope_hi=rope_hi),
        out_shape=jax.ShapeDtypeStruct((s, n), _BF16),
        grid=(s // tm, n // tn),
        in_specs=in_specs,
        out_specs=pl.BlockSpec((tm, tn), lambda i, j: (i, j)),
        scratch_shapes=[pltpu.VMEM((tm, d), _BF16)],
        compiler_params=pltpu.CompilerParams(
            dimension_semantics=("parallel", "arbitrary"),
            vmem_limit_bytes=_vmem_limit(bufs)),
        name=name,
    )(*args)


def _post_mix_kernel(*refs, nparts):
    ys = refs[:nparts]
    wout_ref, h_ref, gmix_ref, gpre_ref, wq_ref, kv_ref, wo_ref, gpost_ref, o_ref = refs[nparts:]
    acc, k0 = None, 0
    for y_ref in ys:
        kk = y_ref.shape[1]
        part = jnp.dot(y_ref[...], wout_ref[k0:k0 + kk, :], preferred_element_type=_F32)
        acc = part if acc is None else acc + part
        k0 += kk
    x = h_ref[...] + _rms(acc, gmix_ref[...])
    hn = _rms(x, gpre_ref[...]).astype(_BF16)
    q = (jnp.dot(hn, wq_ref[...], preferred_element_type=_F32) * _QK_SCALE).astype(_BF16)
    outs = []
    for hd in range(MEM_HEADS):
        sl = slice(hd * HEAD_DIM, (hd + 1) * HEAD_DIM)
        k = kv_ref[:, sl]
        v = kv_ref[:, MEM_W + hd * HEAD_DIM:MEM_W + (hd + 1) * HEAD_DIM]
        sc = lax.dot_general(q[:, sl], k, _NT, preferred_element_type=_F32)
        m = jnp.max(sc, axis=-1, keepdims=True)
        p = jnp.exp2(sc - m)
        l = jnp.sum(p, axis=-1, keepdims=True)
        o = jnp.dot(p.astype(_BF16), v, preferred_element_type=_F32) / l
        outs.append(o.astype(_BF16))
    o_all = jnp.concatenate(outs, axis=-1)
    c = jnp.dot(o_all, wo_ref[...], preferred_element_type=_F32)
    o_ref[...] = x + _rms(c, gpost_ref[...])


def _post_mix(ys, wout, h, gmix, gpre, wq, kv, wo, gpost, *, tm, name):
    s, d = h.shape
    nparts = len(ys)
    row = lambda i: (i, 0)
    fixed = lambda i: (0, 0)
    whole = lambda a: pl.BlockSpec(a.shape, fixed, pipeline_mode=pl.Buffered(1))
    in_specs = [pl.BlockSpec((tm, y.shape[1]), row) for y in ys]
    in_specs += [whole(wout), pl.BlockSpec((tm, d), row), whole(gmix), whole(gpre), whole(wq), whole(kv),
                 whole(wo), whole(gpost)]
    bufs = [((tm, y.shape[1]), _BF16, 2) for y in ys]
    bufs += [(wout.shape, _BF16, 1), (wq.shape, _BF16, 1), (wo.shape, _BF16, 1), (kv.shape, _BF16, 1),
             ((tm, d), _F32, 8), ((tm, d), _BF16, 1), ((tm, MEM_W), _F32, 4)]
    return pl.pallas_call(
        functools.partial(_post_mix_kernel, nparts=nparts),
        out_shape=jax.ShapeDtypeStruct((s, d), _F32),
        grid=(s // tm,),
        in_specs=in_specs,
        out_specs=pl.BlockSpec((tm, d), row),
        compiler_params=pltpu.CompilerParams(
            dimension_semantics=("parallel",),
            vmem_limit_bytes=_vmem_limit(bufs)),
        name=name,
    )(*ys, wout, h, gmix, gpre, wq, kv, wo, gpost)


def _mlp_kernel(h_ref, gpre_ref, wup_ref, wdn_ref, gpost_ref, o_ref, hn_ref, acc_ref):
    c = pl.program_id(1)

    @pl.when(c == 0)
    def _():
        hn_ref[...] = _rms(h_ref[...], gpre_ref[...]).astype(_BF16)
        acc_ref[...] = jnp.zeros(acc_ref.shape, _F32)

    u = jnp.maximum(jnp.dot(hn_ref[...], wup_ref[...], preferred_element_type=_F32), 0.0)
    acc_ref[...] += jnp.dot((u * u).astype(_BF16), wdn_ref[...], preferred_element_type=_F32)

    @pl.when(c == pl.num_programs(1) - 1)
    def _():
        o_ref[...] = h_ref[...] + _rms(acc_ref[...], gpost_ref[...])


def _mlp(h, gpre, wup, wdn, gpost, layer, *, tm, tf, name):
    s, d = h.shape
    f = wup.shape[2]
    bufs = [((tm, d), _F32, 5), ((tm, d), _BF16, 1), ((d, tf), _BF16, 4), ((tm, tf), _F32, 2)]
    return pl.pallas_call(
        _mlp_kernel,
        out_shape=jax.ShapeDtypeStruct((s, d), _F32),
        grid=(s // tm, f // tf),
        in_specs=[pl.BlockSpec((tm, d), lambda i, c: (i, 0)),
                  pl.BlockSpec((1, d), lambda i, c: (0, 0)),
                  pl.BlockSpec((None, d, tf), lambda i, c: (layer, 0, c)),
                  pl.BlockSpec((None, tf, d), lambda i, c: (layer, c, 0)),
                  pl.BlockSpec((1, d), lambda i, c: (0, 0))],
        out_specs=pl.BlockSpec((tm, d), lambda i, c: (i, 0)),
        scratch_shapes=[pltpu.VMEM((tm, d), _BF16), pltpu.VMEM((tm, d), _F32)],
        compiler_params=pltpu.CompilerParams(
            dimension_semantics=("parallel", "arbitrary"),
            vmem_limit_bytes=_vmem_limit(bufs)),
        name=name,
    )(h, gpre, wup, wdn, gpost)


NA_QROWS = 4
NA_BAND = NA_QROWS + NA_KH


def _na_kernel(q_ref, k_ref, vt_ref, bfirst_ref, bmid_ref, blast_ref, o_ref, *, blocks_per_step, n_rows):
    step = pl.program_id(1)
    tq = NA_QROWS * GRID_W
    band = NA_BAND * GRID_W
    scs, vts = [], []
    for u in range(blocks_per_step):
        b = step * blocks_per_step + u
        row0 = jnp.clip(b * NA_QROWS - NA_KH // 2, 0, n_rows - NA_BAND)
        k0 = pl.multiple_of(row0 * GRID_W, NA_QROWS * GRID_W)
        sc = lax.dot_general(k_ref[pl.ds(k0, band), :], q_ref[u * tq:(u + 1) * tq, :], _NT,
                             preferred_element_type=_F32)
        bias = bmid_ref[0]
        if u == 0:
            bias = jnp.where(step == 0, bfirst_ref[0], bias)
        if u == blocks_per_step - 1:
            bias = jnp.where(step == n_rows // (NA_QROWS * blocks_per_step) - 1, blast_ref[0], bias)
        scs.append(sc + bias)
        vts.append(vt_ref[:, pl.ds(k0, band)])
    ps, ls = [], []
    for sc in scs:
        m = jnp.max(sc, axis=0, keepdims=True)
        p = jnp.exp2(sc - m)
        ls.append(jnp.sum(p, axis=0, keepdims=True))
        ps.append(p.astype(_BF16))
    for u in range(blocks_per_step):
        ot = jnp.dot(vts[u], ps[u], preferred_element_type=_F32) / ls[u]
        o_ref[u * tq:(u + 1) * tq, :] = ot.T.astype(_BF16)


def _na_bias_table(rpb, n_rows):
    col = np.arange(GRID_W)
    cs = np.clip(col - NA_KW // 2, 0, GRID_W - NA_KW)
    kc = np.arange(GRID_W)
    col_ok = (kc[:, None] >= cs[None, :]) & (kc[:, None] < cs[None, :] + NA_KW)
    coff = kc[:, None] - col[None, :] + (NA_KW - 1)
    n_ro, n_co = 2 * NA_KH - 1, 2 * NA_KW - 1
    onehot = ((np.arange(n_co)[:, None, None] == coff[None]) & col_ok[None]).astype(np.float32)
    cols = jnp.einsum("hrc,ckj->hrkj", rpb * _LOG2E, onehot, precision=lax.Precision.HIGHEST)
    cols = jnp.where(col_ok[None, None], cols, _MASKED)
    cols = jnp.pad(cols, ((0, 0), (NA_BAND, NA_BAND), (0, 0), (0, 0)), constant_values=_MASKED)
    cols = cols.reshape(NA_HEADS, (n_ro + 2 * NA_BAND) * GRID_W, GRID_W)
    tabs = []
    for i0 in (0, 2 * NA_QROWS, n_rows - NA_QROWS):
        row0 = int(np.clip(i0 - NA_KH // 2, 0, n_rows - NA_BAND))
        qi = i0 + np.arange(NA_QROWS)
        rs = np.clip(qi - NA_KH // 2, 0, n_rows - NA_KH)
        krow = row0 + np.arange(NA_BAND)
        row_ok = (krow[:, None] >= rs[None, :]) & (krow[:, None] < rs[None, :] + NA_KH)
        parts = []
        for qr, q in enumerate(qi):
            lo = (NA_BAND + row0 - int(q) + NA_KH - 1) * GRID_W
            ok = np.repeat(row_ok[:, qr], GRID_W)
            parts.append(jnp.where(ok[None, :, None], cols[:, lo:lo + NA_BAND * GRID_W], _MASKED))
        tabs.append(jnp.concatenate(parts, axis=2).astype(_F32))
    return tabs


def _na_attention(proj, vt, bias_tabs, *, blocks_per_step, name):
    s = proj.shape[0]
    n_rows = s // GRID_W
    assert blocks_per_step >= 2
    tq = blocks_per_step * NA_QROWS * GRID_W
    kcol = NA_W // HEAD_DIM
    bshape = (1, NA_BAND * GRID_W, NA_QROWS * GRID_W)
    bspecs = [pl.BlockSpec(bshape, lambda h, i: (h, 0, 0)) for _ in bias_tabs]
    bufs = [((s, HEAD_DIM), _BF16, 4), (bshape, _F32, 6), ((tq, HEAD_DIM), _BF16, 4),
            ((NA_BAND * GRID_W, NA_QROWS * GRID_W), _F32, 3 * blocks_per_step)]
    return pl.pallas_call(
        functools.partial(_na_kernel, blocks_per_step=blocks_per_step, n_rows=n_rows),
        out_shape=jax.ShapeDtypeStruct((s, NA_W), _BF16),
        grid=(NA_HEADS, n_rows // (NA_QROWS * blocks_per_step)),
        in_specs=[pl.BlockSpec((tq, HEAD_DIM), lambda h, i: (i, h)),
                  pl.BlockSpec((s, HEAD_DIM), lambda h, i: (0, kcol + h)),
                  pl.BlockSpec((HEAD_DIM, s), lambda h, i: (h, 0)),
                  *bspecs],
        out_specs=pl.BlockSpec((tq, HEAD_DIM), lambda h, i: (i, h)),
        compiler_params=pltpu.CompilerParams(
            dimension_semantics=("parallel", "arbitrary"),
            vmem_limit_bytes=_vmem_limit(bufs)),
        name=name,
    )(proj, proj, vt, *bias_tabs)


def _swa_kernel(sink_ref, q_ref, k_ref, vt_ref, b_ref, o_ref, *, tq, band, seq):
    n = pl.program_id(0)
    group = SW_Q_HEADS // SW_KV_HEADS
    start = pl.multiple_of(jnp.clip(n * tq - SW_WINDOW, 0, seq - band), SW_WINDOW)
    bias = b_ref[0]
    scs = []
    for kvh in range(SW_KV_HEADS):
        kb = k_ref[pl.ds(start, band), kvh * HEAD_DIM:(kvh + 1) * HEAD_DIM]
        q = jnp.concatenate([q_ref[:, (kvh * group + g) * HEAD_DIM:(kvh * group + g + 1) * HEAD_DIM]
                             for g in range(group)], axis=0)
        scs.append(lax.dot_general(kb, q, _NT, preferred_element_type=_F32))
    ps, denoms = [], []
    for kvh in range(SW_KV_HEADS):
        pg, dg = [], []
        for g in range(group):
            sink = sink_ref[kvh * group + g]
            sc = scs[kvh][:, g * tq:(g + 1) * tq] + bias
            m = jnp.maximum(jnp.max(sc, axis=0, keepdims=True), sink)
            p = jnp.exp2(sc - m)
            dg.append(jnp.sum(p, axis=0, keepdims=True) + jnp.exp2(sink - m))
            pg.append(p.astype(_BF16))
        ps.append(jnp.concatenate(pg, axis=1))
        denoms.append(jnp.concatenate(dg, axis=1))
    for kvh in range(SW_KV_HEADS):
        vt = vt_ref[kvh * HEAD_DIM:(kvh + 1) * HEAD_DIM, pl.ds(start, band)]
        ot = jnp.dot(vt, ps[kvh], preferred_element_type=_F32) / denoms[kvh]
        for g in range(group):
            hd = kvh * group + g
            o_ref[:, hd * HEAD_DIM:(hd + 1) * HEAD_DIM] = ot[:, g * tq:(g + 1) * tq].T.astype(_BF16)


def _swa_bias_table(tq, band):
    kr = np.arange(band)[:, None]
    qc = np.arange(tq)[None, :]
    tabs = [np.where(np.abs(qc - (rel + kr)) <= SW_WINDOW, 0.0, _MASKED)
            for rel in (0, -SW_WINDOW, tq - band)]
    return jnp.asarray(np.stack(tabs), _F32)


def _swa_attention(proj, vt, sinks_log2, *, tq, name):
    s = proj.shape[0]
    band = tq + 2 * SW_WINDOW
    n_blocks = s // tq
    qcol = 3 * NA_W // SW_QW
    kcol = (3 * NA_W + SW_QW) // SW_KVW
    rows = SW_Q_HEADS // SW_KV_HEADS * tq
    bufs = [((s, SW_KVW), _BF16, 4), ((tq, SW_QW), _BF16, 4), ((band, rows), _F32, 8)]
    variant = lambda n: jnp.where(n == 0, 0, jnp.where(n == n_blocks - 1, 2, 1))
    return pl.pallas_call(
        functools.partial(_swa_kernel, tq=tq, band=band, seq=s),
        out_shape=jax.ShapeDtypeStruct((s, SW_QW), _BF16),
        grid=(n_blocks,),
        in_specs=[pl.BlockSpec(memory_space=pltpu.SMEM),
                  pl.BlockSpec((tq, SW_QW), lambda n: (n, qcol)),
                  pl.BlockSpec((s, SW_KVW), lambda n: (0, kcol)),
                  pl.BlockSpec((SW_KVW, s), lambda n: (0, 0)),
                  pl.BlockSpec((1, band, tq), lambda n: (variant(n), 0, 0))],
        out_specs=pl.BlockSpec((tq, SW_QW), lambda n: (n, 0)),
        compiler_params=pltpu.CompilerParams(
            dimension_semantics=("arbitrary",),
            vmem_limit_bytes=_vmem_limit(bufs)),
        name=name,
    )(sinks_log2, proj, proj, vt, _swa_bias_table(tq, band))


def _diff_kernel(q_ref, k_ref, vt_ref, lq1_ref, lk1_ref, lq2_ref, lk2_ref, g_ref, o_ref,
                 m_ref, l_ref, acc_ref, s_ref, *, tk, seq, lambda_init):
    m_ref[...] = jnp.full(m_ref.shape, -jnp.inf, _F32)
    l_ref[...] = jnp.zeros(l_ref.shape, _F32)
    acc_ref[...] = jnp.zeros(acc_ref.shape, _F32)
    nblk = seq // tk

    def scores(kb, slot):
        k0 = pl.multiple_of(kb * tk, tk)
        for t in range(2):
            sl = slice(t * HEAD_DIM, (t + 1) * HEAD_DIM)
            s_ref[slot + t] = lax.dot_general(k_ref[pl.ds(k0, tk), sl], q_ref[:, sl], _NT,
                                              preferred_element_type=_F32)

    def update(kb, slot):
        k0 = pl.multiple_of(kb * tk, tk)
        vt = vt_ref[:, pl.ds(k0, tk)]
        for t in range(2):
            sc = s_ref[slot + t]
            m_prev = m_ref[t]
            m_new = jnp.maximum(m_prev, jnp.max(sc, axis=0, keepdims=True))
            alpha = jnp.exp2(m_prev - m_new)
            p = jnp.exp2(sc - m_new)
            l_ref[t] = alpha * l_ref[t] + jnp.sum(p, axis=0, keepdims=True)
            acc_ref[t] = alpha * acc_ref[t] + jnp.dot(vt, p.astype(_BF16), preferred_element_type=_F32)
            m_ref[t] = m_new

    scores(0, 0)

    def pair(i, carry):
        scores(2 * i + 1, 2)
        update(2 * i, 0)
        scores(2 * i + 2, 0)
        update(2 * i + 1, 2)
        return carry

    lax.fori_loop(0, nblk // 2 - 1, pair, 0)
    scores(nblk - 1, 2)
    update(nblk - 2, 0)
    update(nblk - 1, 2)

    lam = (jnp.exp(jnp.sum(lq1_ref[...] * lk1_ref[...], axis=-1, keepdims=True))
           - jnp.exp(jnp.sum(lq2_ref[...] * lk2_ref[...], axis=-1, keepdims=True)) + lambda_init)
    ot = acc_ref[0] / l_ref[0] - lam * (acc_ref[1] / l_ref[1])
    o_ref[...] = (_rms(ot.T, g_ref[...]) * (1.0 - lambda_init)).astype(_BF16)


def _diff_attention(proj, vt, lq1, lk1, lq2, lk2, subln_g, lambda_init, *, tq, tk, name):
    s = proj.shape[0]
    hw = 2 * HEAD_DIM
    kcol = DIFF_W // hw
    vec = pl.BlockSpec((1, HEAD_DIM), lambda h, qb: (0, 0))
    bufs = [((s, hw), _BF16, 4), ((tq, hw), _BF16, 4), ((tk, tq), _F32, 6),
            ((2, hw, tq), _F32, 2)]
    return pl.pallas_call(
        functools.partial(_diff_kernel, tk=tk, seq=s, lambda_init=lambda_init),
        out_shape=jax.ShapeDtypeStruct((s, DIFF_W), _BF16),
        grid=(DIFF_HEADS, s // tq),
        in_specs=[pl.BlockSpec((tq, hw), lambda h, qb: (qb, h)),
                  pl.BlockSpec((s, hw), lambda h, qb: (0, kcol + h)),
                  pl.BlockSpec((hw, s), lambda h, qb: (h, 0)),
                  vec, vec, vec, vec,
                  pl.BlockSpec((1, hw), lambda h, qb: (0, 0))],
        out_specs=pl.BlockSpec((tq, hw), lambda h, qb: (qb, h)),
        scratch_shapes=[pltpu.VMEM((2, 1, tq), _F32), pltpu.VMEM((2, 1, tq), _F32),
                        pltpu.VMEM((2, hw, tq), _F32), pltpu.VMEM((4, tk, tq), _F32)],
        compiler_params=pltpu.CompilerParams(
            dimension_semantics=("parallel", "arbitrary"),
            vmem_limit_bytes=_vmem_limit(bufs)),
        name=name,
    )(proj, proj, vt, lq1, lk1, lq2, lk2, subln_g)


def _rope_tables(seq):
    half = ROPE_DIM // 2
    inv = 1.0 / (ROPE_THETA ** (np.arange(0, ROPE_DIM, 2, dtype=np.float64) / ROPE_DIM))
    ang = np.arange(seq, dtype=np.float64)[:, None] * inv[None, :]
    cos, sin = np.cos(ang), np.sin(ang)
    pad = V7X_LANES - ROPE_DIM
    c = np.concatenate([cos, cos, np.ones((seq, pad))], axis=1)
    s1 = np.concatenate([-sin, np.zeros((seq, pad + half))], axis=1)
    s2 = np.concatenate([np.zeros((seq, half)), sin, np.zeros((seq, pad))], axis=1)
    return tuple(jnp.asarray(t, _F32) for t in (c, s1, s2))


def _col_scale(n, ranges):
    cs = np.ones((1, n), np.float32)
    for lo, hi in ranges:
        cs[:, lo:hi] = _QK_SCALE
    return jnp.asarray(cs)


def kernel(x, mem, even_w_in, even_w_out, na_rpb, sw_sinks, odd_w_in, odd_w_out, diff_lam_q1, diff_lam_k1, diff_lam_q2, diff_lam_k2, diff_subln_g, mix_pre_g, mix_post_g, mem_norm_g, mem_pre_g, mem_post_g, mem_wq, mem_wk, mem_wv, mem_wo, mlp_pre_g, mlp_post_g, mlp_w_up, mlp_w_down):
    b, s, d = x.shape
    assert (b, s, d) == (1, SEQ, D_MODEL)
    h = x.reshape(s, d)
    mem2 = mem.reshape(MEM_LEN, d)
    rope_tabs = _rope_tables(s)
    row = lambda v: v.reshape(1, -1)
    w_up, w_down = mlp_w_up.astype(_BF16), mlp_w_down.astype(_BF16)

    for layer in range(DEPTH):
        if layer % 2 == 0:
            e = layer // 2
            sw_q0 = 3 * NA_W
            proj = _norm_matmul(
                h, row(mix_pre_g[layer]), even_w_in[e],
                _col_scale(EVEN_IN, [(0, NA_W), (sw_q0, sw_q0 + SW_QW)]),
                rope_tabs, (sw_q0, sw_q0 + SW_QW + SW_KVW), tm=1024, tn=512, name=f"in_proj{layer}")
            out_a = _na_attention(proj, proj[:, 2 * NA_W:3 * NA_W].T, _na_bias_table(na_rpb[e], s // GRID_W),
                                  blocks_per_step=8, name=f"na{layer}")
            out_b = _swa_attention(proj, proj[:, EVEN_IN - SW_KVW:].T, sw_sinks[e] * _LOG2E,
                                   tq=256, name=f"swa{layer}")
            ys, w_out = [out_a, out_b], even_w_out[e].astype(_BF16)
        else:
            o = layer // 2
            lambda_init = 0.8 - 0.6 * math.exp(-0.3 * layer)
            proj = _norm_matmul(
                h, row(mix_pre_g[layer]), odd_w_in[o],
                _col_scale(ODD_IN, [(0, DIFF_W)]),
                rope_tabs, (0, 2 * DIFF_W), tm=1024, tn=512, name=f"in_proj{layer}")
            y = _diff_attention(proj, proj[:, 2 * DIFF_W:].T, row(diff_lam_q1[o]), row(diff_lam_k1[o]), row(diff_lam_q2[o]),
                                row(diff_lam_k2[o]), row(diff_subln_g[o]), lambda_init,
                                tq=2048, tk=512, name=f"diff{layer}")
            ys, w_out = [y], odd_w_out[o].astype(_BF16)

        wkv = jnp.concatenate([mem_wk[layer], mem_wv[layer]], axis=1).astype(_BF16)
        kv = _norm_matmul(mem2, row(mem_norm_g[layer]), wkv, jnp.ones((1, 2 * MEM_W), _F32),
                          tm=MEM_LEN, tn=512, name=f"mem_kv{layer}")
        h = _post_mix(ys, w_out, h, row(mix_post_g[layer]), row(mem_pre_g[layer]),
                      mem_wq[layer].astype(_BF16), kv, mem_wo[layer].astype(_BF16),
                      row(mem_post_g[layer]), tm=512, name=f"post_mix{layer}")
        h = _mlp(h, row(mlp_pre_g[layer]), w_up, w_down, row(mlp_post_g[layer]), layer,
                 tm=512, tf=1024, name=f"mlp{layer}")
    return h.reshape(b, s, d)
```

```python
import functools
import math

import jax
import jax.numpy as jnp
import numpy as np
from jax import lax
from jax.experimental import pallas as pl
from jax.experimental.pallas import tpu as pltpu

D_MODEL = 2048
SEQ = 8192
DEPTH = 2
GRID_W = 64
HEAD_DIM = 128
ROPE_THETA = 500000.0
ROPE_DIM = HEAD_DIM // 4
NORM_EPS = 1e-6
NA_HEADS = 8
NA_KH = 8
NA_KW = 16
SW_Q_HEADS = 8
SW_KV_HEADS = 2
SW_WINDOW = 128
DIFF_HEADS = 8
MEM_LEN = 256
MEM_HEADS = 4
D_FF = 4 * D_MODEL
NA_W = NA_HEADS * HEAD_DIM
SW_QW = SW_Q_HEADS * HEAD_DIM
SW_KVW = SW_KV_HEADS * HEAD_DIM
EVEN_IN = 3 * NA_W + SW_QW + 2 * SW_KVW
DIFF_W = DIFF_HEADS * 2 * HEAD_DIM
ODD_IN = 3 * DIFF_W
MEM_W = MEM_HEADS * HEAD_DIM

V7X_LANES = 128
V7X_VMEM_BYTES = 64 * 1024 * 1024

_F32 = jnp.float32
_BF16 = jnp.bfloat16
_LOG2E = 1.4426950408889634
_QK_SCALE = HEAD_DIM ** -0.5 * _LOG2E
_MASKED = -1e30
_NT = (((1,), (1,)), ((), ()))


def _nbytes(shape, dtype):
    return math.prod(shape) * jnp.dtype(dtype).itemsize


def _vmem_limit(buffers):
    need = sum(_nbytes(s, d) * n for s, d, n in buffers)
    return min(V7X_VMEM_BYTES - (4 << 20), need + (8 << 20))


def _rms(x, g):
    ms = jnp.mean(x * x, axis=-1, keepdims=True)
    return x * lax.rsqrt(ms + NORM_EPS) * g


def _rope_rot(a, c, s1, s2):
    return (a * c + pltpu.roll(a, V7X_LANES - ROPE_DIM // 2, 1) * s1
            + pltpu.roll(a, ROPE_DIM // 2, 1) * s2)


def _norm_matmul_kernel(x_ref, g_ref, w_ref, cs_ref, *rest, nchunk, rope_lo, rope_hi):
    if rope_lo is None:
        o_ref, hn_ref = rest
    else:
        c_ref, s1_ref, s2_ref, o_ref, hn_ref = rest
    j = pl.program_id(1)

    @pl.when(j == 0)
    def _():
        hn_ref[...] = _rms(x_ref[...], g_ref[...]).astype(_BF16)

    acc = jnp.dot(hn_ref[...], w_ref[...].astype(_BF16), preferred_element_type=_F32) * cs_ref[...]
    for c in range(nchunk):
        sl = slice(c * V7X_LANES, (c + 1) * V7X_LANES)
        a = acc[:, sl]
        if rope_lo is None:
            o_ref[:, sl] = a.astype(_BF16)
        else:
            gc = j * nchunk + c
            is_rope = jnp.logical_and(gc >= rope_lo, gc < rope_hi)

            @pl.when(is_rope)
            def _():
                o_ref[:, sl] = _rope_rot(a, c_ref[...], s1_ref[...], s2_ref[...]).astype(_BF16)

            @pl.when(jnp.logical_not(is_rope))
            def _():
                o_ref[:, sl] = a.astype(_BF16)


def _norm_matmul(x, g, w, colscale, rope_tabs=None, rope_cols=None, *, tm, tn, name):
    s, d = x.shape
    n = w.shape[1]
    nchunk = tn // V7X_LANES
    in_specs = [
        pl.BlockSpec((tm, d), lambda i, j: (i, 0)),
        pl.BlockSpec((1, d), lambda i, j: (0, 0)),
        pl.BlockSpec((d, tn), lambda i, j: (0, j)),
        pl.BlockSpec((1, tn), lambda i, j: (0, j)),
    ]
    args = [x, g, w, colscale]
    rope_lo = rope_hi = None
    if rope_cols is not None:
        rope_lo, rope_hi = rope_cols[0] // V7X_LANES, rope_cols[1] // V7X_LANES
        in_specs += [pl.BlockSpec((tm, V7X_LANES), lambda i, j: (i, 0))] * 3
        args += list(rope_tabs)
    bufs = [((tm, d), _F32, 3), ((d, tn), w.dtype, 2), ((d, tn), _BF16, 1), ((tm, tn), _F32, 2),
            ((tm, d), _BF16, 1)]
    return pl.pallas_call(
        functools.partial(_norm_matmul_kernel, nchunk=nchunk, rope_lo=rope_lo, rope_hi=rope_hi),
        out_shape=jax.ShapeDtypeStruct((s, n), _BF16),
        grid=(s // tm, n // tn),
        in_specs=in_specs,
        out_specs=pl.BlockSpec((tm, tn), lambda i, j: (i, j)),
        scratch_shapes=[pltpu.VMEM((tm, d), _BF16)],
        compiler_params=pltpu.CompilerParams(
            dimension_semantics=("parallel", "arbitrary"),
            vmem_limit_bytes=_vmem_limit(bufs)),
        name=name,
    )(*args)


def _post_mix_kernel(*refs, nparts):
    ys = refs[:nparts]
    wout_ref, h_ref, gmix_ref, gpre_ref, wq_ref, kv_ref, wo_ref, gpost_ref, o_ref = refs[nparts:]
    acc, k0 = None, 0
    for y_ref in ys:
        kk = y_ref.shape[1]
        part = jnp.dot(y_ref[...], wout_ref[k0:k0 + kk, :], preferred_element_type=_F32)
        acc = part if acc is None else acc + part
        k0 += kk
    x = h_ref[...] + _rms(acc, gmix_ref[...])
    hn = _rms(x, gpre_ref[...]).astype(_BF16)
    q = (jnp.dot(hn, wq_ref[...], preferred_element_type=_F32) * _QK_SCALE).astype(_BF16)
    outs = []
    for hd in range(MEM_HEADS):
        sl = slice(hd * HEAD_DIM, (hd + 1) * HEAD_DIM)
        k = kv_ref[:, sl]
        v = kv_ref[:, MEM_W + hd * HEAD_DIM:MEM_W + (hd + 1) * HEAD_DIM]
        sc = lax.dot_general(q[:, sl], k, _NT, preferred_element_type=_F32)
        m = jnp.max(sc, axis=-1, keepdims=True)
        p = jnp.exp2(sc - m)
        l = jnp.sum(p, axis=-1, keepdims=True)
        o = jnp.dot(p.astype(_BF16), v, preferred_element_type=_F32) / l
        outs.append(o.astype(_BF16))
    o_all = jnp.concatenate(outs, axis=-1)
    c = jnp.dot(o_all, wo_ref[...], preferred_element_type=_F32)
    o_ref[...] = x + _rms(c, gpost_ref[...])


def _post_mix(ys, wout, h, gmix, gpre, wq, kv, wo, gpost, *, tm, name):
    s, d = h.shape
    nparts = len(ys)
    row = lambda i: (i, 0)
    fixed = lambda i: (0, 0)
    whole = lambda a: pl.BlockSpec(a.shape, fixed, pipeline_mode=pl.Buffered(1))
    in_specs = [pl.BlockSpec((tm, y.shape[1]), row) for y in ys]
    in_specs += [whole(wout), pl.BlockSpec((tm, d), row), whole(gmix), whole(gpre), whole(wq), whole(kv),
                 whole(wo), whole(gpost)]
    bufs = [((tm, y.shape[1]), _BF16, 2) for y in ys]
    bufs += [(wout.shape, _BF16, 1), (wq.shape, _BF16, 1), (wo.shape, _BF16, 1), (kv.shape, _BF16, 1),
             ((tm, d), _F32, 8), ((tm, d), _BF16, 1), ((tm, MEM_W), _F32, 4)]
    return pl.pallas_call(
        functools.partial(_post_mix_kernel, nparts=nparts),
        out_shape=jax.ShapeDtypeStruct((s, d), _F32),
        grid=(s // tm,),
        in_specs=in_specs,
        out_specs=pl.BlockSpec((tm, d), row),
        compiler_params=pltpu.CompilerParams(
            dimension_semantics=("parallel",),
            vmem_limit_bytes=_vmem_limit(bufs)),
        name=name,
    )(*ys, wout, h, gmix, gpre, wq, kv, wo, gpost)


def _mlp_kernel(h_ref, gpre_ref, wup_ref, wdn_ref, gpost_ref, o_ref, hn_ref):
    c = pl.program_id(1)

    @pl.when(c == 0)
    def _():
        hn_ref[...] = _rms(h_ref[...], gpre_ref[...]).astype(_BF16)
        o_ref[...] = jnp.zeros(o_ref.shape, _F32)

    u = jnp.maximum(jnp.dot(hn_ref[...], wup_ref[...].astype(_BF16), preferred_element_type=_F32), 0.0)
    o_ref[...] += jnp.dot((u * u).astype(_BF16), wdn_ref[...].astype(_BF16), preferred_element_type=_F32)

    @pl.when(c == pl.num_programs(1) - 1)
    def _():
        o_ref[...] = h_ref[...] + _rms(o_ref[...], gpost_ref[...])


def _mlp(h, gpre, wup, wdn, gpost, layer, *, tm, tf, name):
    s, d = h.shape
    f = wup.shape[2]
    bufs = [((tm, d), _F32, 3), ((tm, d), _BF16, 1), ((d, tf), wup.dtype, 4), ((d, tf), _BF16, 2),
            ((tm, tf), _F32, 2)]
    return pl.pallas_call(
        _mlp_kernel,
        out_shape=jax.ShapeDtypeStruct((s, d), _F32),
        grid=(s // tm, f // tf),
        in_specs=[pl.BlockSpec((tm, d), lambda i, c: (i, 0), pipeline_mode=pl.Buffered(1)),
                  pl.BlockSpec((1, d), lambda i, c: (0, 0)),
                  pl.BlockSpec((None, d, tf), lambda i, c: (layer, 0, c)),
                  pl.BlockSpec((None, tf, d), lambda i, c: (layer, c, 0)),
                  pl.BlockSpec((1, d), lambda i, c: (0, 0))],
        out_specs=pl.BlockSpec((tm, d), lambda i, c: (i, 0)),
        scratch_shapes=[pltpu.VMEM((tm, d), _BF16)],
        compiler_params=pltpu.CompilerParams(
            dimension_semantics=("parallel", "arbitrary"),
            vmem_limit_bytes=_vmem_limit(bufs)),
        name=name,
    )(h, gpre, wup, wdn, gpost)


NA_QROWS = 4
NA_BAND = NA_QROWS + NA_KH


def _na_kernel(q_ref, k_ref, vt_ref, bfirst_ref, bmid_ref, blast_ref, o_ref, *, blocks_per_step, n_rows):
    step = pl.program_id(1)
    tq = NA_QROWS * GRID_W
    band = NA_BAND * GRID_W
    scs, vts = [], []
    for u in range(blocks_per_step):
        b = step * blocks_per_step + u
        row0 = jnp.clip(b * NA_QROWS - NA_KH // 2, 0, n_rows - NA_BAND)
        k0 = pl.multiple_of(row0 * GRID_W, NA_QROWS * GRID_W)
        sc = lax.dot_general(k_ref[pl.ds(k0, band), :], q_ref[u * tq:(u + 1) * tq, :], _NT,
                             preferred_element_type=_F32)
        bias = bmid_ref[0]
        if u == 0:
            bias = jnp.where(step == 0, bfirst_ref[0], bias)
        if u == blocks_per_step - 1:
            bias = jnp.where(step == n_rows // (NA_QROWS * blocks_per_step) - 1, blast_ref[0], bias)
        scs.append(sc + bias)
        vts.append(vt_ref[:, pl.ds(k0, band)])
    ps, ls = [], []
    for sc in scs:
        m = jnp.max(sc, axis=0, keepdims=True)
        p = jnp.exp2(sc - m)
        ls.append(jnp.sum(p, axis=0, keepdims=True))
        ps.append(p.astype(_BF16))
    for u in range(blocks_per_step):
        ot = jnp.dot(vts[u], ps[u], preferred_element_type=_F32) / ls[u]
        o_ref[u * tq:(u + 1) * tq, :] = ot.T.astype(_BF16)


def _na_bias_table(rpb, n_rows):
    col = np.arange(GRID_W)
    cs = np.clip(col - NA_KW // 2, 0, GRID_W - NA_KW)
    kc = np.arange(GRID_W)
    col_ok = (kc[:, None] >= cs[None, :]) & (kc[:, None] < cs[None, :] + NA_KW)
    coff = kc[:, None] - col[None, :] + (NA_KW - 1)
    n_ro, n_co = 2 * NA_KH - 1, 2 * NA_KW - 1
    onehot = ((np.arange(n_co)[:, None, None] == coff[None]) & col_ok[None]).astype(np.float32)
    cols = jnp.einsum("hrc,ckj->hrkj", rpb * _LOG2E, onehot, precision=lax.Precision.HIGHEST)
    cols = jnp.where(col_ok[None, None], cols, _MASKED)
    cols = jnp.pad(cols, ((0, 0), (NA_BAND, NA_BAND), (0, 0), (0, 0)), constant_values=_MASKED)
    cols = cols.reshape(NA_HEADS, (n_ro + 2 * NA_BAND) * GRID_W, GRID_W)
    tabs = []
    for i0 in (0, 2 * NA_QROWS, n_rows - NA_QROWS):
        row0 = int(np.clip(i0 - NA_KH // 2, 0, n_rows - NA_BAND))
        qi = i0 + np.arange(NA_QROWS)
        rs = np.clip(qi - NA_KH // 2, 0, n_rows - NA_KH)
        krow = row0 + np.arange(NA_BAND)
        row_ok = (krow[:, None] >= rs[None, :]) & (krow[:, None] < rs[None, :] + NA_KH)
        parts = []
        for qr, q in enumerate(qi):
            lo = (NA_BAND + row0 - int(q) + NA_KH - 1) * GRID_W
            ok = np.repeat(row_ok[:, qr], GRID_W)
            parts.append(jnp.where(ok[None, :, None], cols[:, lo:lo + NA_BAND * GRID_W], _MASKED))
        tabs.append(jnp.concatenate(parts, axis=2).astype(_F32))
    return tabs


def _na_attention(proj, vt, bias_tabs, *, blocks_per_step, name):
    s = proj.shape[0]
    n_rows = s // GRID_W
    assert blocks_per_step >= 2
    tq = blocks_per_step * NA_QROWS * GRID_W
    kcol = NA_W // HEAD_DIM
    bshape = (1, NA_BAND * GRID_W, NA_QROWS * GRID_W)
    bspecs = [pl.BlockSpec(bshape, lambda h, i: (h, 0, 0)) for _ in bias_tabs]
    bufs = [((s, HEAD_DIM), _BF16, 4), (bshape, _F32, 6), ((tq, HEAD_DIM), _BF16, 4),
            ((NA_BAND * GRID_W, NA_QROWS * GRID_W), _F32, 3 * blocks_per_step)]
    return pl.pallas_call(
        functools.partial(_na_kernel, blocks_per_step=blocks_per_step, n_rows=n_rows),
        out_shape=jax.ShapeDtypeStruct((s, NA_W), _BF16),
        grid=(NA_HEADS, n_rows // (NA_QROWS * blocks_per_step)),
        in_specs=[pl.BlockSpec((tq, HEAD_DIM), lambda h, i: (i, h)),
                  pl.BlockSpec((s, HEAD_DIM), lambda h, i: (0, kcol + h)),
                  pl.BlockSpec((HEAD_DIM, s), lambda h, i: (h, 0)),
                  *bspecs],
        out_specs=pl.BlockSpec((tq, HEAD_DIM), lambda h, i: (i, h)),
        compiler_params=pltpu.CompilerParams(
            dimension_semantics=("parallel", "arbitrary"),
            vmem_limit_bytes=_vmem_limit(bufs)),
        name=name,
    )(proj, proj, vt, *bias_tabs)


def _swa_kernel(sink_ref, q_ref, k_ref, vt_ref, b_ref, o_ref, *, tq, band, seq):
    n = pl.program_id(0)
    group = SW_Q_HEADS // SW_KV_HEADS
    start = pl.multiple_of(jnp.clip(n * tq - SW_WINDOW, 0, seq - band), SW_WINDOW)
    bias = b_ref[0]
    scs = []
    for kvh in range(SW_KV_HEADS):
        kb = k_ref[pl.ds(start, band), kvh * HEAD_DIM:(kvh + 1) * HEAD_DIM]
        q = jnp.concatenate([q_ref[:, (kvh * group + g) * HEAD_DIM:(kvh * group + g + 1) * HEAD_DIM]
                             for g in range(group)], axis=0)
        scs.append(lax.dot_general(kb, q, _NT, preferred_element_type=_F32))
    ps, denoms = [], []
    for kvh in range(SW_KV_HEADS):
        pg, dg = [], []
        for g in range(group):
            sink = sink_ref[kvh * group + g]
            sc = scs[kvh][:, g * tq:(g + 1) * tq] + bias
            m = jnp.maximum(jnp.max(sc, axis=0, keepdims=True), sink)
            p = jnp.exp2(sc - m)
            dg.append(jnp.sum(p, axis=0, keepdims=True) + jnp.exp2(sink - m))
            pg.append(p.astype(_BF16))
        ps.append(jnp.concatenate(pg, axis=1))
        denoms.append(jnp.concatenate(dg, axis=1))
    for kvh in range(SW_KV_HEADS):
        vt = vt_ref[kvh * HEAD_DIM:(kvh + 1) * HEAD_DIM, pl.ds(start, band)]
        ot = jnp.dot(vt, ps[kvh], preferred_element_type=_F32) / denoms[kvh]
        for g in range(group):
            hd = kvh * group + g
            o_ref[:, hd * HEAD_DIM:(hd + 1) * HEAD_DIM] = ot[:, g * tq:(g + 1) * tq].T.astype(_BF16)


def _swa_bias_table(tq, band):
    kr = np.arange(band)[:, None]
    qc = np.arange(tq)[None, :]
    tabs = [np.where(np.abs(qc - (rel + kr)) <= SW_WINDOW, 0.0, _MASKED)
            for rel in (0, -SW_WINDOW, tq - band)]
    return jnp.asarray(np.stack(tabs), _F32)


def _swa_attention(proj, vt, sinks_log2, *, tq, name):
    s = proj.shape[0]
    band = tq + 2 * SW_WINDOW
    n_blocks = s // tq
    qcol = 3 * NA_W // SW_QW
    kcol = (3 * NA_W + SW_QW) // SW_KVW
    rows = SW_Q_HEADS // SW_KV_HEADS * tq
    bufs = [((s, SW_KVW), _BF16, 4), ((tq, SW_QW), _BF16, 4), ((band, rows), _F32, 8)]
    variant = lambda n: jnp.where(n == 0, 0, jnp.where(n == n_blocks - 1, 2, 1))
    return pl.pallas_call(
        functools.partial(_swa_kernel, tq=tq, band=band, seq=s),
        out_shape=jax.ShapeDtypeStruct((s, SW_QW), _BF16),
        grid=(n_blocks,),
        in_specs=[pl.BlockSpec(memory_space=pltpu.SMEM),
                  pl.BlockSpec((tq, SW_QW), lambda n: (n, qcol)),
                  pl.BlockSpec((s, SW_KVW), lambda n: (0, kcol)),
                  pl.BlockSpec((SW_KVW, s), lambda n: (0, 0)),
                  pl.BlockSpec((1, band, tq), lambda n: (variant(n), 0, 0))],
        out_specs=pl.BlockSpec((tq, SW_QW), lambda n: (n, 0)),
        compiler_params=pltpu.CompilerParams(
            dimension_semantics=("arbitrary",),
            vmem_limit_bytes=_vmem_limit(bufs)),
        name=name,
    )(sinks_log2, proj, proj, vt, _swa_bias_table(tq, band))


def _diff_kernel(q_ref, k_ref, vt_ref, lq1_ref, lk1_ref, lq2_ref, lk2_ref, g_ref, o_ref,
                 m_ref, l_ref, acc_ref, s_ref, *, tk, seq, lambda_init):
    m_ref[...] = jnp.full(m_ref.shape, -jnp.inf, _F32)
    l_ref[...] = jnp.zeros(l_ref.shape, _F32)
    acc_ref[...] = jnp.zeros(acc_ref.shape, _F32)
    nblk = seq // tk

    def scores(kb, slot):
        k0 = pl.multiple_of(kb * tk, tk)
        for t in range(2):
            sl = slice(t * HEAD_DIM, (t + 1) * HEAD_DIM)
            s_ref[slot + t] = lax.dot_general(k_ref[pl.ds(k0, tk), sl], q_ref[:, sl], _NT,
                                              preferred_element_type=_F32)

    def update(kb, slot):
        k0 = pl.multiple_of(kb * tk, tk)
        vt = vt_ref[:, pl.ds(k0, tk)]
        for t in range(2):
            sc = s_ref[slot + t]
            m_prev = m_ref[t]
            m_new = jnp.maximum(m_prev, jnp.max(sc, axis=0, keepdims=True))
            alpha = jnp.exp2(m_prev - m_new)
            p = jnp.exp2(sc - m_new)
            l_ref[t] = alpha * l_ref[t] + jnp.sum(p, axis=0, keepdims=True)
            acc_ref[t] = alpha * acc_ref[t] + jnp.dot(vt, p.astype(_BF16), preferred_element_type=_F32)
            m_ref[t] = m_new

    scores(0, 0)

    def pair(i, carry):
        scores(2 * i + 1, 2)
        update(2 * i, 0)
        scores(2 * i + 2, 0)
        update(2 * i + 1, 2)
        return carry

    lax.fori_loop(0, nblk // 2 - 1, pair, 0)
    scores(nblk - 1, 2)
    update(nblk - 2, 0)
    update(nblk - 1, 2)

    lam = (jnp.exp(jnp.sum(lq1_ref[...] * lk1_ref[...], axis=-1, keepdims=True))
           - jnp.exp(jnp.sum(lq2_ref[...] * lk2_ref[...], axis=-1, keepdims=True)) + lambda_init)
    ot = acc_ref[0] / l_ref[0] - lam * (acc_ref[1] / l_ref[1])
    o_ref[...] = (_rms(ot.T, g_ref[...]) * (1.0 - lambda_init)).astype(_BF16)


def _diff_attention(proj, vt, lq1, lk1, lq2, lk2, subln_g, lambda_init, *, tq, tk, name):
    s = proj.shape[0]
    hw = 2 * HEAD_DIM
    kcol = DIFF_W // hw
    vec = pl.BlockSpec((1, HEAD_DIM), lambda h, qb: (0, 0))
    bufs = [((s, hw), _BF16, 4), ((tq, hw), _BF16, 4), ((tk, tq), _F32, 6),
            ((2, hw, tq), _F32, 2)]
    return pl.pallas_call(
        functools.partial(_diff_kernel, tk=tk, seq=s, lambda_init=lambda_init),
        out_shape=jax.ShapeDtypeStruct((s, DIFF_W), _BF16),
        grid=(DIFF_HEADS, s // tq),
        in_specs=[pl.BlockSpec((tq, hw), lambda h, qb: (qb, h)),
                  pl.BlockSpec((s, hw), lambda h, qb: (0, kcol + h)),
                  pl.BlockSpec((hw, s), lambda h, qb: (h, 0)),
                  vec, vec, vec, vec,
                  pl.BlockSpec((1, hw), lambda h, qb: (0, 0))],
        out_specs=pl.BlockSpec((tq, hw), lambda h, qb: (qb, h)),
        scratch_shapes=[pltpu.VMEM((2, 1, tq), _F32), pltpu.VMEM((2, 1, tq), _F32),
                        pltpu.VMEM((2, hw, tq), _F32), pltpu.VMEM((4, tk, tq), _F32)],
        compiler_params=pltpu.CompilerParams(
            dimension_semantics=("parallel", "arbitrary"),
            vmem_limit_bytes=_vmem_limit(bufs)),
        name=name,
    )(proj, proj, vt, lq1, lk1, lq2, lk2, subln_g)


def _rope_tables(seq):
    half = ROPE_DIM // 2
    inv = 1.0 / (ROPE_THETA ** (np.arange(0, ROPE_DIM, 2, dtype=np.float64) / ROPE_DIM))
    ang = np.arange(seq, dtype=np.float64)[:, None] * inv[None, :]
    cos, sin = np.cos(ang), np.sin(ang)
    pad = V7X_LANES - ROPE_DIM
    c = np.concatenate([cos, cos, np.ones((seq, pad))], axis=1)
    s1 = np.concatenate([-sin, np.zeros((seq, pad + half))], axis=1)
    s2 = np.concatenate([np.zeros((seq, half)), sin, np.zeros((seq, pad))], axis=1)
    return tuple(jnp.asarray(t, _F32) for t in (c, s1, s2))


def _col_scale(n, ranges):
    cs = np.ones((1, n), np.float32)
    for lo, hi in ranges:
        cs[:, lo:hi] = _QK_SCALE
    return jnp.asarray(cs)


def kernel(x, mem, even_w_in, even_w_out, na_rpb, sw_sinks, odd_w_in, odd_w_out, diff_lam_q1, diff_lam_k1, diff_lam_q2, diff_lam_k2, diff_subln_g, mix_pre_g, mix_post_g, mem_norm_g, mem_pre_g, mem_post_g, mem_wq, mem_wk, mem_wv, mem_wo, mlp_pre_g, mlp_post_g, mlp_w_up, mlp_w_down):
    b, s, d = x.shape
    assert (b, s, d) == (1, SEQ, D_MODEL)
    h = x.reshape(s, d)
    mem2 = mem.reshape(MEM_LEN, d)
    rope_tabs = _rope_tables(s)
    row = lambda v: v.reshape(1, -1)
    w_up, w_down = mlp_w_up, mlp_w_down

    for layer in range(DEPTH):
        if layer % 2 == 0:
            e = layer // 2
            sw_q0 = 3 * NA_W
            proj = _norm_matmul(
                h, row(mix_pre_g[layer]), even_w_in[e],
                _col_scale(EVEN_IN, [(0, NA_W), (sw_q0, sw_q0 + SW_QW)]),
                rope_tabs, (sw_q0, sw_q0 + SW_QW + SW_KVW), tm=1024, tn=512, name=f"in_proj{layer}")
            out_a = _na_attention(proj, proj[:, 2 * NA_W:3 * NA_W].T, _na_bias_table(na_rpb[e], s // GRID_W),
                                  blocks_per_step=8, name=f"na{layer}")
            out_b = _swa_attention(proj, proj[:, EVEN_IN - SW_KVW:].T, sw_sinks[e] * _LOG2E,
                                   tq=256, name=f"swa{layer}")
            ys, w_out = [out_a, out_b], even_w_out[e].astype(_BF16)
        else:
            o = layer // 2
            lambda_init = 0.8 - 0.6 * math.exp(-0.3 * layer)
            proj = _norm_matmul(
                h, row(mix_pre_g[layer]), odd_w_in[o],
                _col_scale(ODD_IN, [(0, DIFF_W)]),
                rope_tabs, (0, 2 * DIFF_W), tm=1024, tn=512, name=f"in_proj{layer}")
            y = _diff_attention(proj, proj[:, 2 * DIFF_W:].T, row(diff_lam_q1[o]), row(diff_lam_k1[o]), row(diff_lam_q2[o]),
                                row(diff_lam_k2[o]), row(diff_subln_g[o]), lambda_init,
                                tq=2048, tk=512, name=f"diff{layer}")
            ys, w_out = [y], odd_w_out[o].astype(_BF16)

        wkv = jnp.concatenate([mem_wk[layer], mem_wv[layer]], axis=1).astype(_BF16)
        kv = _norm_matmul(mem2, row(mem_norm_g[layer]), wkv, jnp.ones((1, 2 * MEM_W), _F32),
                          tm=MEM_LEN, tn=512, name=f"mem_kv{layer}")
        h = _post_mix(ys, w_out, h, row(mix_post_g[layer]), row(mem_pre_g[layer]),
                      mem_wq[layer].astype(_BF16), kv, mem_wo[layer].astype(_BF16),
                      row(mem_post_g[layer]), tm=512, name=f"post_mix{layer}")
        h = _mlp(h, row(mlp_pre_g[layer]), w_up, w_down, row(mlp_post_g[layer]), layer,
                 tm=1024, tf=512, name=f"mlp{layer}")
    return h.reshape(b, s, d)
```

```python
import functools
import math

import jax
import jax.numpy as jnp
import numpy as np
from jax import lax
from jax.experimental import pallas as pl
from jax.experimental.pallas import tpu as pltpu

D_MODEL = 2048
SEQ = 8192
DEPTH = 2
GRID_W = 64
HEAD_DIM = 128
ROPE_THETA = 500000.0
ROPE_DIM = HEAD_DIM // 4
NORM_EPS = 1e-6
NA_HEADS = 8
NA_KH = 8
NA_KW = 16
SW_Q_HEADS = 8
SW_KV_HEADS = 2
SW_WINDOW = 128
DIFF_HEADS = 8
MEM_LEN = 256
MEM_HEADS = 4
D_FF = 4 * D_MODEL
NA_W = NA_HEADS * HEAD_DIM
SW_QW = SW_Q_HEADS * HEAD_DIM
SW_KVW = SW_KV_HEADS * HEAD_DIM
EVEN_IN = 3 * NA_W + SW_QW + 2 * SW_KVW
DIFF_W = DIFF_HEADS * 2 * HEAD_DIM
ODD_IN = 3 * DIFF_W
MEM_W = MEM_HEADS * HEAD_DIM

V7X_LANES = 128
V7X_VMEM_BYTES = 64 * 1024 * 1024

_F32 = jnp.float32
_BF16 = jnp.bfloat16
_LOG2E = 1.4426950408889634
_QK_SCALE = HEAD_DIM ** -0.5 * _LOG2E
_MASKED = -1e30
_NT = (((1,), (1,)), ((), ()))


def _nbytes(shape, dtype):
    return math.prod(shape) * jnp.dtype(dtype).itemsize


def _vmem_limit(buffers):
    need = sum(_nbytes(s, d) * n for s, d, n in buffers)
    return min(V7X_VMEM_BYTES - (4 << 20), need + (8 << 20))


def _rms(x, g):
    ms = jnp.mean(x * x, axis=-1, keepdims=True)
    return x * lax.rsqrt(ms + NORM_EPS) * g


def _rope_rot(a, c, s1, s2):
    return (a * c + pltpu.roll(a, V7X_LANES - ROPE_DIM // 2, 1) * s1
            + pltpu.roll(a, ROPE_DIM // 2, 1) * s2)


def _norm_matmul_kernel(x_ref, g_ref, w_ref, cs_ref, *rest, nchunk, mrows, rope_lo, rope_hi):
    if rope_lo is None:
        o_ref, hn_ref = rest
    else:
        c_ref, s1_ref, s2_ref, o_ref, hn_ref = rest
    j = pl.program_id(1)

    @pl.when(j == 0)
    def _():
        hn_ref[...] = _rms(x_ref[...], g_ref[...]).astype(_BF16)

    w = w_ref[...].astype(_BF16)
    cs = cs_ref[...]
    for r in range(x_ref.shape[0] // mrows):
        rows = slice(r * mrows, (r + 1) * mrows)
        acc = jnp.dot(hn_ref[rows, :], w, preferred_element_type=_F32) * cs
        for c in range(nchunk):
            sl = slice(c * V7X_LANES, (c + 1) * V7X_LANES)
            a = acc[:, sl]
            if rope_lo is not None:
                gc = j * nchunk + c
                is_rope = jnp.logical_and(gc >= rope_lo, gc < rope_hi)
                a = _rope_rot(a, jnp.where(is_rope, c_ref[rows, :], 1.0),
                              jnp.where(is_rope, s1_ref[rows, :], 0.0),
                              jnp.where(is_rope, s2_ref[rows, :], 0.0))
            o_ref[rows, sl] = a.astype(_BF16)


def _norm_matmul(x, g, w, colscale, rope_tabs=None, rope_cols=None, *, tm, tn, name):
    s, d = x.shape
    n = w.shape[1]
    nchunk = tn // V7X_LANES
    in_specs = [
        pl.BlockSpec((tm, d), lambda i, j: (i, 0)),
        pl.BlockSpec((1, d), lambda i, j: (0, 0)),
        pl.BlockSpec((d, tn), lambda i, j: (0, j)),
        pl.BlockSpec((1, tn), lambda i, j: (0, j)),
    ]
    args = [x, g, w, colscale]
    rope_lo = rope_hi = None
    if rope_cols is not None:
        rope_lo, rope_hi = rope_cols[0] // V7X_LANES, rope_cols[1] // V7X_LANES
        in_specs += [pl.BlockSpec((tm, V7X_LANES), lambda i, j: (i, 0))] * 3
        args += list(rope_tabs)
    bufs = [((tm, d), _F32, 3), ((d, tn), w.dtype, 2), ((d, tn), _BF16, 1), ((tm, tn), _F32, 2),
            ((tm, d), _BF16, 1)]
    return pl.pallas_call(
        functools.partial(_norm_matmul_kernel, nchunk=nchunk, mrows=min(tm, 256), rope_lo=rope_lo,
                          rope_hi=rope_hi),
        out_shape=jax.ShapeDtypeStruct((s, n), _BF16),
        grid=(s // tm, n // tn),
        in_specs=in_specs,
        out_specs=pl.BlockSpec((tm, tn), lambda i, j: (i, j)),
        scratch_shapes=[pltpu.VMEM((tm, d), _BF16)],
        compiler_params=pltpu.CompilerParams(
            dimension_semantics=("parallel", "arbitrary"),
            vmem_limit_bytes=_vmem_limit(bufs)),
        name=name,
    )(*args)


def _post_mix_kernel(*refs, nparts):
    ys = refs[:nparts]
    wout_ref, h_ref, gmix_ref, gpre_ref, wq_ref, kv_ref, wo_ref, gpost_ref, o_ref = refs[nparts:]
    acc, k0 = None, 0
    for y_ref in ys:
        kk = y_ref.shape[1]
        part = jnp.dot(y_ref[...], wout_ref[k0:k0 + kk, :], preferred_element_type=_F32)
        acc = part if acc is None else acc + part
        k0 += kk
    x = h_ref[...] + _rms(acc, gmix_ref[...])
    hn = _rms(x, gpre_ref[...]).astype(_BF16)
    q = (jnp.dot(hn, wq_ref[...], preferred_element_type=_F32) * _QK_SCALE).astype(_BF16)
    outs = []
    for hd in range(MEM_HEADS):
        sl = slice(hd * HEAD_DIM, (hd + 1) * HEAD_DIM)
        k = kv_ref[:, sl]
        v = kv_ref[:, MEM_W + hd * HEAD_DIM:MEM_W + (hd + 1) * HEAD_DIM]
        sc = lax.dot_general(q[:, sl], k, _NT, preferred_element_type=_F32)
        m = jnp.max(sc, axis=-1, keepdims=True)
        p = jnp.exp2(sc - m)
        l = jnp.sum(p, axis=-1, keepdims=True)
        o = jnp.dot(p.astype(_BF16), v, preferred_element_type=_F32) / l
        outs.append(o.astype(_BF16))
    o_all = jnp.concatenate(outs, axis=-1)
    c = jnp.dot(o_all, wo_ref[...], preferred_element_type=_F32)
    o_ref[...] = x + _rms(c, gpost_ref[...])


def _post_mix(ys, wout, h, gmix, gpre, wq, kv, wo, gpost, *, tm, name):
    s, d = h.shape
    nparts = len(ys)
    row = lambda i: (i, 0)
    fixed = lambda i: (0, 0)
    whole = lambda a: pl.BlockSpec(a.shape, fixed, pipeline_mode=pl.Buffered(1))
    in_specs = [pl.BlockSpec((tm, y.shape[1]), row) for y in ys]
    in_specs += [whole(wout), pl.BlockSpec((tm, d), row), whole(gmix), whole(gpre), whole(wq), whole(kv),
                 whole(wo), whole(gpost)]
    bufs = [((tm, y.shape[1]), _BF16, 2) for y in ys]
    bufs += [(wout.shape, _BF16, 1), (wq.shape, _BF16, 1), (wo.shape, _BF16, 1), (kv.shape, _BF16, 1),
             ((tm, d), _F32, 8), ((tm, d), _BF16, 1), ((tm, MEM_W), _F32, 4)]
    return pl.pallas_call(
        functools.partial(_post_mix_kernel, nparts=nparts),
        out_shape=jax.ShapeDtypeStruct((s, d), _F32),
        grid=(s // tm,),
        in_specs=in_specs,
        out_specs=pl.BlockSpec((tm, d), row),
        compiler_params=pltpu.CompilerParams(
            dimension_semantics=("parallel",),
            vmem_limit_bytes=_vmem_limit(bufs)),
        name=name,
    )(*ys, wout, h, gmix, gpre, wq, kv, wo, gpost)


def _mlp_kernel(h_ref, gpre_ref, wup_ref, wdn_ref, gpost_ref, o_ref, hn_ref, acc_ref):
    c = pl.program_id(1)

    @pl.when(c == 0)
    def _():
        hn_ref[...] = _rms(h_ref[...], gpre_ref[...]).astype(_BF16)
        acc_ref[...] = jnp.zeros(acc_ref.shape, _F32)

    u = jnp.maximum(jnp.dot(hn_ref[...], wup_ref[...], preferred_element_type=_F32), 0.0)
    acc_ref[...] += jnp.dot((u * u).astype(_BF16), wdn_ref[...], preferred_element_type=_F32)

    @pl.when(c == pl.num_programs(1) - 1)
    def _():
        o_ref[...] = h_ref[...] + _rms(acc_ref[...], gpost_ref[...])


def _mlp(h, gpre, wup, wdn, gpost, layer, *, tm, tf, name):
    s, d = h.shape
    f = wup.shape[2]
    bufs = [((tm, d), _F32, 5), ((tm, d), _BF16, 1), ((d, tf), _BF16, 4), ((tm, tf), _F32, 2)]
    return pl.pallas_call(
        _mlp_kernel,
        out_shape=jax.ShapeDtypeStruct((s, d), _F32),
        grid=(s // tm, f // tf),
        in_specs=[pl.BlockSpec((tm, d), lambda i, c: (i, 0)),
                  pl.BlockSpec((1, d), lambda i, c: (0, 0)),
                  pl.BlockSpec((None, d, tf), lambda i, c: (layer, 0, c)),
                  pl.BlockSpec((None, tf, d), lambda i, c: (layer, c, 0)),
                  pl.BlockSpec((1, d), lambda i, c: (0, 0))],
        out_specs=pl.BlockSpec((tm, d), lambda i, c: (i, 0)),
        scratch_shapes=[pltpu.VMEM((tm, d), _BF16), pltpu.VMEM((tm, d), _F32)],
        compiler_params=pltpu.CompilerParams(
            dimension_semantics=("parallel", "arbitrary"),
            vmem_limit_bytes=_vmem_limit(bufs)),
        name=name,
    )(h, gpre, wup, wdn, gpost)


NA_QROWS = 4
NA_BAND = NA_QROWS + NA_KH


def _na_kernel(q_ref, k_ref, vt_ref, bfirst_ref, bmid_ref, blast_ref, o_ref, *, blocks_per_step, n_rows):
    step = pl.program_id(1)
    tq = NA_QROWS * GRID_W
    band = NA_BAND * GRID_W
    scs, vts = [], []
    for u in range(blocks_per_step):
        b = step * blocks_per_step + u
        row0 = jnp.clip(b * NA_QROWS - NA_KH // 2, 0, n_rows - NA_BAND)
        k0 = pl.multiple_of(row0 * GRID_W, NA_QROWS * GRID_W)
        sc = lax.dot_general(k_ref[pl.ds(k0, band), :], q_ref[u * tq:(u + 1) * tq, :], _NT,
                             preferred_element_type=_F32)
        bias = bmid_ref[0]
        if u == 0:
            bias = jnp.where(step == 0, bfirst_ref[0], bias)
        if u == blocks_per_step - 1:
            bias = jnp.where(step == n_rows // (NA_QROWS * blocks_per_step) - 1, blast_ref[0], bias)
        scs.append(sc + bias)
        vts.append(vt_ref[:, pl.ds(k0, band)])
    ps, ls = [], []
    for sc in scs:
        m = jnp.max(sc, axis=0, keepdims=True)
        p = jnp.exp2(sc - m)
        ls.append(jnp.sum(p, axis=0, keepdims=True))
        ps.append(p.astype(_BF16))
    for u in range(blocks_per_step):
        ot = jnp.dot(vts[u], ps[u], preferred_element_type=_F32) / ls[u]
        o_ref[u * tq:(u + 1) * tq, :] = ot.T.astype(_BF16)


def _na_bias_table(rpb, n_rows):
    col = np.arange(GRID_W)
    cs = np.clip(col - NA_KW // 2, 0, GRID_W - NA_KW)
    kc = np.arange(GRID_W)
    col_ok = (kc[:, None] >= cs[None, :]) & (kc[:, None] < cs[None, :] + NA_KW)
    coff = kc[:, None] - col[None, :] + (NA_KW - 1)
    n_ro, n_co = 2 * NA_KH - 1, 2 * NA_KW - 1
    onehot = ((np.arange(n_co)[:, None, None] == coff[None]) & col_ok[None]).astype(np.float32)
    cols = jnp.einsum("hrc,ckj->hrkj", rpb * _LOG2E, onehot, precision=lax.Precision.HIGHEST)
    cols = jnp.where(col_ok[None, None], cols, _MASKED)
    cols = jnp.pad(cols, ((0, 0), (NA_BAND, NA_BAND), (0, 0), (0, 0)), constant_values=_MASKED)
    cols = cols.reshape(NA_HEADS, (n_ro + 2 * NA_BAND) * GRID_W, GRID_W)
    tabs = []
    for i0 in (0, 2 * NA_QROWS, n_rows - NA_QROWS):
        row0 = int(np.clip(i0 - NA_KH // 2, 0, n_rows - NA_BAND))
        qi = i0 + np.arange(NA_QROWS)
        rs = np.clip(qi - NA_KH // 2, 0, n_rows - NA_KH)
        krow = row0 + np.arange(NA_BAND)
        row_ok = (krow[:, None] >= rs[None, :]) & (krow[:, None] < rs[None, :] + NA_KH)
        parts = []
        for qr, q in enumerate(qi):
            lo = (NA_BAND + row0 - int(q) + NA_KH - 1) * GRID_W
            ok = np.repeat(row_ok[:, qr], GRID_W)
            parts.append(jnp.where(ok[None, :, None], cols[:, lo:lo + NA_BAND * GRID_W], _MASKED))
        tabs.append(jnp.concatenate(parts, axis=2).astype(_F32))
    return tabs


def _na_attention(proj, vt, bias_tabs, *, blocks_per_step, name):
    s = proj.shape[0]
    n_rows = s // GRID_W
    assert blocks_per_step >= 2
    tq = blocks_per_step * NA_QROWS * GRID_W
    kcol = NA_W // HEAD_DIM
    bshape = (1, NA_BAND * GRID_W, NA_QROWS * GRID_W)
    bspecs = [pl.BlockSpec(bshape, lambda h, i: (h, 0, 0)) for _ in bias_tabs]
    bufs = [((s, HEAD_DIM), _BF16, 4), (bshape, _F32, 6), ((tq, HEAD_DIM), _BF16, 4),
            ((NA_BAND * GRID_W, NA_QROWS * GRID_W), _F32, 3 * blocks_per_step)]
    return pl.pallas_call(
        functools.partial(_na_kernel, blocks_per_step=blocks_per_step, n_rows=n_rows),
        out_shape=jax.ShapeDtypeStruct((s, NA_W), _BF16),
        grid=(NA_HEADS, n_rows // (NA_QROWS * blocks_per_step)),
        in_specs=[pl.BlockSpec((tq, HEAD_DIM), lambda h, i: (i, h)),
                  pl.BlockSpec((s, HEAD_DIM), lambda h, i: (0, kcol + h)),
                  pl.BlockSpec((HEAD_DIM, s), lambda h, i: (h, 0)),
                  *bspecs],
        out_specs=pl.BlockSpec((tq, HEAD_DIM), lambda h, i: (i, h)),
        compiler_params=pltpu.CompilerParams(
            dimension_semantics=("parallel", "arbitrary"),
            vmem_limit_bytes=_vmem_limit(bufs)),
        name=name,
    )(proj, proj, vt, *bias_tabs)


def _swa_kernel(sink_ref, q_ref, k_ref, vt_ref, b_ref, o_ref, *, tq, band, seq):
    n = pl.program_id(0)
    group = SW_Q_HEADS // SW_KV_HEADS
    start = pl.multiple_of(jnp.clip(n * tq - SW_WINDOW, 0, seq - band), SW_WINDOW)
    bias = b_ref[0]
    scs = []
    for kvh in range(SW_KV_HEADS):
        kb = k_ref[pl.ds(start, band), kvh * HEAD_DIM:(kvh + 1) * HEAD_DIM]
        q = jnp.concatenate([q_ref[:, (kvh * group + g) * HEAD_DIM:(kvh * group + g + 1) * HEAD_DIM]
                             for g in range(group)], axis=0)
        scs.append(lax.dot_general(kb, q, _NT, preferred_element_type=_F32))
    ps, denoms = [], []
    for kvh in range(SW_KV_HEADS):
        pg, dg = [], []
        for g in range(group):
            sink = sink_ref[kvh * group + g]
            sc = scs[kvh][:, g * tq:(g + 1) * tq] + bias
            m = jnp.maximum(jnp.max(sc, axis=0, keepdims=True), sink)
            p = jnp.exp2(sc - m)
            dg.append(jnp.sum(p, axis=0, keepdims=True) + jnp.exp2(sink - m))
            pg.append(p.astype(_BF16))
        ps.append(jnp.concatenate(pg, axis=1))
        denoms.append(jnp.concatenate(dg, axis=1))
    for kvh in range(SW_KV_HEADS):
        vt = vt_ref[kvh * HEAD_DIM:(kvh + 1) * HEAD_DIM, pl.ds(start, band)]
        ot = jnp.dot(vt, ps[kvh], preferred_element_type=_F32) / denoms[kvh]
        for g in range(group):
            hd = kvh * group + g
            o_ref[:, hd * HEAD_DIM:(hd + 1) * HEAD_DIM] = ot[:, g * tq:(g + 1) * tq].T.astype(_BF16)


def _swa_bias_table(tq, band):
    kr = np.arange(band)[:, None]
    qc = np.arange(tq)[None, :]
    tabs = [np.where(np.abs(qc - (rel + kr)) <= SW_WINDOW, 0.0, _MASKED)
            for rel in (0, -SW_WINDOW, tq - band)]
    return jnp.asarray(np.stack(tabs), _F32)


def _swa_attention(proj, vt, sinks_log2, *, tq, name):
    s = proj.shape[0]
    band = tq + 2 * SW_WINDOW
    n_blocks = s // tq
    qcol = 3 * NA_W // SW_QW
    kcol = (3 * NA_W + SW_QW) // SW_KVW
    rows = SW_Q_HEADS // SW_KV_HEADS * tq
    bufs = [((s, SW_KVW), _BF16, 4), ((tq, SW_QW), _BF16, 4), ((band, rows), _F32, 8)]
    variant = lambda n: jnp.where(n == 0, 0, jnp.where(n == n_blocks - 1, 2, 1))
    return pl.pallas_call(
        functools.partial(_swa_kernel, tq=tq, band=band, seq=s),
        out_shape=jax.ShapeDtypeStruct((s, SW_QW), _BF16),
        grid=(n_blocks,),
        in_specs=[pl.BlockSpec(memory_space=pltpu.SMEM),
                  pl.BlockSpec((tq, SW_QW), lambda n: (n, qcol)),
                  pl.BlockSpec((s, SW_KVW), lambda n: (0, kcol)),
                  pl.BlockSpec((SW_KVW, s), lambda n: (0, 0)),
                  pl.BlockSpec((1, band, tq), lambda n: (variant(n), 0, 0))],
        out_specs=pl.BlockSpec((tq, SW_QW), lambda n: (n, 0)),
        compiler_params=pltpu.CompilerParams(
            dimension_semantics=("arbitrary",),
            vmem_limit_bytes=_vmem_limit(bufs)),
        name=name,
    )(sinks_log2, proj, proj, vt, _swa_bias_table(tq, band))


def _diff_kernel(q_ref, k_ref, vt_ref, lq1_ref, lk1_ref, lq2_ref, lk2_ref, g_ref, o_ref,
                 m_ref, l_ref, acc_ref, s_ref, *, tk, seq, lambda_init):
    m_ref[...] = jnp.full(m_ref.shape, -jnp.inf, _F32)
    l_ref[...] = jnp.zeros(l_ref.shape, _F32)
    acc_ref[...] = jnp.zeros(acc_ref.shape, _F32)
    nblk = seq // tk

    def scores(kb, slot):
        k0 = pl.multiple_of(kb * tk, tk)
        for t in range(2):
            sl = slice(t * HEAD_DIM, (t + 1) * HEAD_DIM)
            s_ref[slot + t] = lax.dot_general(k_ref[pl.ds(k0, tk), sl], q_ref[:, sl], _NT,
                                              preferred_element_type=_F32)

    def update(kb, slot):
        k0 = pl.multiple_of(kb * tk, tk)
        vt = vt_ref[:, pl.ds(k0, tk)]
        for t in range(2):
            sc = s_ref[slot + t]
            m_prev = m_ref[t]
            m_new = jnp.maximum(m_prev, jnp.max(sc, axis=0, keepdims=True))
            alpha = jnp.exp2(m_prev - m_new)
            p = jnp.exp2(sc - m_new)
            l_ref[t] = alpha * l_ref[t] + jnp.sum(p, axis=0, keepdims=True)
            acc_ref[t] = alpha * acc_ref[t] + jnp.dot(vt, p.astype(_BF16), preferred_element_type=_F32)
            m_ref[t] = m_new

    scores(0, 0)

    def pair(i, carry):
        scores(2 * i + 1, 2)
        update(2 * i, 0)
        scores(2 * i + 2, 0)
        update(2 * i + 1, 2)
        return carry

    lax.fori_loop(0, nblk // 2 - 1, pair, 0)
    scores(nblk - 1, 2)
    update(nblk - 2, 0)
    update(nblk - 1, 2)

    lam = (jnp.exp(jnp.sum(lq1_ref[...] * lk1_ref[...], axis=-1, keepdims=True))
           - jnp.exp(jnp.sum(lq2_ref[...] * lk2_ref[...], axis=-1, keepdims=True)) + lambda_init)
    ot = acc_ref[0] / l_ref[0] - lam * (acc_ref[1] / l_ref[1])
    o_ref[...] = (_rms(ot.T, g_ref[...]) * (1.0 - lambda_init)).astype(_BF16)


def _diff_attention(proj, vt, lq1, lk1, lq2, lk2, subln_g, lambda_init, *, tq, tk, name):
    s = proj.shape[0]
    hw = 2 * HEAD_DIM
    kcol = DIFF_W // hw
    vec = pl.BlockSpec((1, HEAD_DIM), lambda h, qb: (0, 0))
    bufs = [((s, hw), _BF16, 4), ((tq, hw), _BF16, 4), ((tk, tq), _F32, 6),
            ((2, hw, tq), _F32, 2)]
    return pl.pallas_call(
        functools.partial(_diff_kernel, tk=tk, seq=s, lambda_init=lambda_init),
        out_shape=jax.ShapeDtypeStruct((s, DIFF_W), _BF16),
        grid=(DIFF_HEADS, s // tq),
        in_specs=[pl.BlockSpec((tq, hw), lambda h, qb: (qb, h)),
                  pl.BlockSpec((s, hw), lambda h, qb: (0, kcol + h)),
                  pl.BlockSpec((hw, s), lambda h, qb: (h, 0)),
                  vec, vec, vec, vec,
                  pl.BlockSpec((1, hw), lambda h, qb: (0, 0))],
        out_specs=pl.BlockSpec((tq, hw), lambda h, qb: (qb, h)),
        scratch_shapes=[pltpu.VMEM((2, 1, tq), _F32), pltpu.VMEM((2, 1, tq), _F32),
                        pltpu.VMEM((2, hw, tq), _F32), pltpu.VMEM((4, tk, tq), _F32)],
        compiler_params=pltpu.CompilerParams(
            dimension_semantics=("parallel", "arbitrary"),
            vmem_limit_bytes=_vmem_limit(bufs)),
        name=name,
    )(proj, proj, vt, lq1, lk1, lq2, lk2, subln_g)


def _rope_tables(seq):
    half = ROPE_DIM // 2
    inv = 1.0 / (ROPE_THETA ** (np.arange(0, ROPE_DIM, 2, dtype=np.float64) / ROPE_DIM))
    ang = np.arange(seq, dtype=np.float64)[:, None] * inv[None, :]
    cos, sin = np.cos(ang), np.sin(ang)
    pad = V7X_LANES - ROPE_DIM
    c = np.concatenate([cos, cos, np.ones((seq, pad))], axis=1)
    s1 = np.concatenate([-sin, np.zeros((seq, pad + half))], axis=1)
    s2 = np.concatenate([np.zeros((seq, half)), sin, np.zeros((seq, pad))], axis=1)
    return tuple(jnp.asarray(t, _F32) for t in (c, s1, s2))


def _col_scale(n, ranges):
    cs = np.ones((1, n), np.float32)
    for lo, hi in ranges:
        cs[:, lo:hi] = _QK_SCALE
    return jnp.asarray(cs)


def kernel(x, mem, even_w_in, even_w_out, na_rpb, sw_sinks, odd_w_in, odd_w_out, diff_lam_q1, diff_lam_k1, diff_lam_q2, diff_lam_k2, diff_subln_g, mix_pre_g, mix_post_g, mem_norm_g, mem_pre_g, mem_post_g, mem_wq, mem_wk, mem_wv, mem_wo, mlp_pre_g, mlp_post_g, mlp_w_up, mlp_w_down):
    b, s, d = x.shape
    assert (b, s, d) == (1, SEQ, D_MODEL)
    h = x.reshape(s, d)
    mem2 = mem.reshape(MEM_LEN, d)
    rope_tabs = _rope_tables(s)
    row = lambda v: v.reshape(1, -1)
    w_up, w_down = mlp_w_up.astype(_BF16), mlp_w_down.astype(_BF16)

    for layer in range(DEPTH):
        if layer % 2 == 0:
            e = layer // 2
            sw_q0 = 3 * NA_W
            proj = _norm_matmul(
                h, row(mix_pre_g[layer]), even_w_in[e],
                _col_scale(EVEN_IN, [(0, NA_W), (sw_q0, sw_q0 + SW_QW)]),
                rope_tabs, (sw_q0, sw_q0 + SW_QW + SW_KVW), tm=1024, tn=512, name=f"in_proj{layer}")
            out_a = _na_attention(proj, proj[:, 2 * NA_W:3 * NA_W].T, _na_bias_table(na_rpb[e], s // GRID_W),
                                  blocks_per_step=8, name=f"na{layer}")
            out_b = _swa_attention(proj, proj[:, EVEN_IN - SW_KVW:].T, sw_sinks[e] * _LOG2E,
                                   tq=256, name=f"swa{layer}")
            ys, w_out = [out_a, out_b], even_w_out[e].astype(_BF16)
        else:
            o = layer // 2
            lambda_init = 0.8 - 0.6 * math.exp(-0.3 * layer)
            proj = _norm_matmul(
                h, row(mix_pre_g[layer]), odd_w_in[o],
                _col_scale(ODD_IN, [(0, DIFF_W)]),
                rope_tabs, (0, 2 * DIFF_W), tm=1024, tn=512, name=f"in_proj{layer}")
            y = _diff_attention(proj, proj[:, 2 * DIFF_W:].T, row(diff_lam_q1[o]), row(diff_lam_k1[o]), row(diff_lam_q2[o]),
                                row(diff_lam_k2[o]), row(diff_subln_g[o]), lambda_init,
                                tq=2048, tk=512, name=f"diff{layer}")
            ys, w_out = [y], odd_w_out[o].astype(_BF16)

        wkv = jnp.concatenate([mem_wk[layer], mem_wv[layer]], axis=1).astype(_BF16)
        kv = _norm_matmul(mem2, row(mem_norm_g[layer]), wkv, jnp.ones((1, 2 * MEM_W), _F32),
                          tm=MEM_LEN, tn=512, name=f"mem_kv{layer}")
        h = _post_mix(ys, w_out, h, row(mix_post_g[layer]), row(mem_pre_g[layer]),
                      mem_wq[layer].astype(_BF16), kv, mem_wo[layer].astype(_BF16),
                      row(mem_post_g[layer]), tm=512, name=f"post_mix{layer}")
        h = _mlp(h, row(mlp_pre_g[layer]), w_up, w_down, row(mlp_post_g[layer]), layer,
                 tm=512, tf=1024, name=f"mlp{layer}")
    return h.reshape(b, s, d)
```

```python
import functools
import math

import jax
import jax.numpy as jnp
import numpy as np
from jax import lax
from jax.experimental import pallas as pl
from jax.experimental.pallas import tpu as pltpu

D_MODEL = 2048
SEQ = 8192
DEPTH = 2
GRID_W = 64
HEAD_DIM = 128
ROPE_THETA = 500000.0
ROPE_DIM = HEAD_DIM // 4
NORM_EPS = 1e-6
NA_HEADS = 8
NA_KH = 8
NA_KW = 16
SW_Q_HEADS = 8
SW_KV_HEADS = 2
SW_WINDOW = 128
DIFF_HEADS = 8
MEM_LEN = 256
MEM_HEADS = 4
D_FF = 4 * D_MODEL
NA_W = NA_HEADS * HEAD_DIM
SW_QW = SW_Q_HEADS * HEAD_DIM
SW_KVW = SW_KV_HEADS * HEAD_DIM
EVEN_IN = 3 * NA_W + SW_QW + 2 * SW_KVW
DIFF_W = DIFF_HEADS * 2 * HEAD_DIM
ODD_IN = 3 * DIFF_W
MEM_W = MEM_HEADS * HEAD_DIM

V7X_LANES = 128
V7X_VMEM_BYTES = 64 * 1024 * 1024

_F32 = jnp.float32
_BF16 = jnp.bfloat16
_LOG2E = 1.4426950408889634
_QK_SCALE = HEAD_DIM ** -0.5 * _LOG2E
_MASKED = -1e30
_NT = (((1,), (1,)), ((), ()))


def _nbytes(shape, dtype):
    return math.prod(shape) * jnp.dtype(dtype).itemsize


def _vmem_limit(buffers):
    need = sum(_nbytes(s, d) * n for s, d, n in buffers)
    return min(V7X_VMEM_BYTES - (4 << 20), need + (8 << 20))


def _rms(x, g):
    ms = jnp.mean(x * x, axis=-1, keepdims=True)
    return x * lax.rsqrt(ms + NORM_EPS) * g


def _rope_rot(a, c, s1, s2):
    return (a * c + pltpu.roll(a, V7X_LANES - ROPE_DIM // 2, 1) * s1
            + pltpu.roll(a, ROPE_DIM // 2, 1) * s2)


def _norm_matmul_kernel(x_ref, g_ref, w_ref, cs_ref, *rest, nchunk, mrows, rope_lo, rope_hi):
    if rope_lo is None:
        o_ref, hn_ref = rest
    else:
        c_ref, s1_ref, s2_ref, o_ref, hn_ref = rest
    j = pl.program_id(1)

    @pl.when(j == 0)
    def _():
        hn_ref[...] = _rms(x_ref[...], g_ref[...]).astype(_BF16)

    w = w_ref[...].astype(_BF16)
    cs = cs_ref[...]
    for r in range(x_ref.shape[0] // mrows):
        rows = slice(r * mrows, (r + 1) * mrows)
        acc = jnp.dot(hn_ref[rows, :], w, preferred_element_type=_F32) * cs
        for c in range(nchunk):
            sl = slice(c * V7X_LANES, (c + 1) * V7X_LANES)
            a = acc[:, sl]
            if rope_lo is not None:
                gc = j * nchunk + c
                is_rope = jnp.logical_and(gc >= rope_lo, gc < rope_hi)
                a = _rope_rot(a, jnp.where(is_rope, c_ref[rows, :], 1.0),
                              jnp.where(is_rope, s1_ref[rows, :], 0.0),
                              jnp.where(is_rope, s2_ref[rows, :], 0.0))
            o_ref[rows, sl] = a.astype(_BF16)


def _norm_matmul(x, g, w, colscale, rope_tabs=None, rope_cols=None, *, tm, tn, name):
    s, d = x.shape
    n = w.shape[1]
    nchunk = tn // V7X_LANES
    in_specs = [
        pl.BlockSpec((tm, d), lambda i, j: (i, 0)),
        pl.BlockSpec((1, d), lambda i, j: (0, 0)),
        pl.BlockSpec((d, tn), lambda i, j: (0, j)),
        pl.BlockSpec((1, tn), lambda i, j: (0, j)),
    ]
    args = [x, g, w, colscale]
    rope_lo = rope_hi = None
    if rope_cols is not None:
        rope_lo, rope_hi = rope_cols[0] // V7X_LANES, rope_cols[1] // V7X_LANES
        in_specs += [pl.BlockSpec((tm, V7X_LANES), lambda i, j: (i, 0))] * 3
        args += list(rope_tabs)
    bufs = [((tm, d), _F32, 3), ((d, tn), w.dtype, 2), ((d, tn), _BF16, 1), ((tm, tn), _F32, 2),
            ((tm, d), _BF16, 1)]
    return pl.pallas_call(
        functools.partial(_norm_matmul_kernel, nchunk=nchunk, mrows=min(tm, 256), rope_lo=rope_lo,
                          rope_hi=rope_hi),
        out_shape=jax.ShapeDtypeStruct((s, n), _BF16),
        grid=(s // tm, n // tn),
        in_specs=in_specs,
        out_specs=pl.BlockSpec((tm, tn), lambda i, j: (i, j)),
        scratch_shapes=[pltpu.VMEM((tm, d), _BF16)],
        compiler_params=pltpu.CompilerParams(
            dimension_semantics=("parallel", "arbitrary"),
            vmem_limit_bytes=_vmem_limit(bufs)),
        name=name,
    )(*args)


def _post_mix_kernel(*refs, nparts):
    ys = refs[:nparts]
    wout_ref, h_ref, gmix_ref, gpre_ref, wq_ref, kv_ref, wo_ref, gpost_ref, o_ref = refs[nparts:]
    acc, k0 = None, 0
    for y_ref in ys:
        kk = y_ref.shape[1]
        part = jnp.dot(y_ref[...], wout_ref[k0:k0 + kk, :], preferred_element_type=_F32)
        acc = part if acc is None else acc + part
        k0 += kk
    x = h_ref[...] + _rms(acc, gmix_ref[...])
    hn = _rms(x, gpre_ref[...]).astype(_BF16)
    q = (jnp.dot(hn, wq_ref[...], preferred_element_type=_F32) * _QK_SCALE).astype(_BF16)
    outs = []
    for hd in range(MEM_HEADS):
        sl = slice(hd * HEAD_DIM, (hd + 1) * HEAD_DIM)
        k = kv_ref[:, sl]
        v = kv_ref[:, MEM_W + hd * HEAD_DIM:MEM_W + (hd + 1) * HEAD_DIM]
        sc = lax.dot_general(q[:, sl], k, _NT, preferred_element_type=_F32)
        m = jnp.max(sc, axis=-1, keepdims=True)
        p = jnp.exp2(sc - m)
        l = jnp.sum(p, axis=-1, keepdims=True)
        o = jnp.dot(p.astype(_BF16), v, preferred_element_type=_F32) / l
        outs.append(o.astype(_BF16))
    o_all = jnp.concatenate(outs, axis=-1)
    c = jnp.dot(o_all, wo_ref[...], preferred_element_type=_F32)
    o_ref[...] = x + _rms(c, gpost_ref[...])


def _post_mix(ys, wout, h, gmix, gpre, wq, kv, wo, gpost, *, tm, name):
    s, d = h.shape
    nparts = len(ys)
    row = lambda i: (i, 0)
    fixed = lambda i: (0, 0)
    whole = lambda a: pl.BlockSpec(a.shape, fixed, pipeline_mode=pl.Buffered(1))
    in_specs = [pl.BlockSpec((tm, y.shape[1]), row) for y in ys]
    in_specs += [whole(wout), pl.BlockSpec((tm, d), row), whole(gmix), whole(gpre), whole(wq), whole(kv),
                 whole(wo), whole(gpost)]
    bufs = [((tm, y.shape[1]), _BF16, 2) for y in ys]
    bufs += [(wout.shape, _BF16, 1), (wq.shape, _BF16, 1), (wo.shape, _BF16, 1), (kv.shape, _BF16, 1),
             ((tm, d), _F32, 8), ((tm, d), _BF16, 1), ((tm, MEM_W), _F32, 4)]
    return pl.pallas_call(
        functools.partial(_post_mix_kernel, nparts=nparts),
        out_shape=jax.ShapeDtypeStruct((s, d), _F32),
        grid=(s // tm,),
        in_specs=in_specs,
        out_specs=pl.BlockSpec((tm, d), row),
        compiler_params=pltpu.CompilerParams(
            dimension_semantics=("parallel",),
            vmem_limit_bytes=_vmem_limit(bufs)),
        name=name,
    )(*ys, wout, h, gmix, gpre, wq, kv, wo, gpost)


def _mlp_kernel(h_ref, gpre_ref, wup_ref, wdn_ref, gpost_ref, o_ref, hn_ref, acc_ref):
    c = pl.program_id(1)

    @pl.when(c == 0)
    def _():
        hn_ref[...] = _rms(h_ref[...], gpre_ref[...]).astype(_BF16)
        acc_ref[...] = jnp.zeros(acc_ref.shape, _F32)

    u = jnp.maximum(jnp.dot(hn_ref[...], wup_ref[...], preferred_element_type=_F32), 0.0)
    acc_ref[...] += jnp.dot((u * u).astype(_BF16), wdn_ref[...], preferred_element_type=_F32)

    @pl.when(c == pl.num_programs(1) - 1)
    def _():
        o_ref[...] = h_ref[...] + _rms(acc_ref[...], gpost_ref[...])


def _mlp(h, gpre, wup, wdn, gpost, layer, *, tm, tf, name):
    s, d = h.shape
    f = wup.shape[2]
    bufs = [((tm, d), _F32, 5), ((tm, d), _BF16, 1), ((d, tf), _BF16, 4), ((tm, tf), _F32, 2)]
    return pl.pallas_call(
        _mlp_kernel,
        out_shape=jax.ShapeDtypeStruct((s, d), _F32),
        grid=(s // tm, f // tf),
        in_specs=[pl.BlockSpec((tm, d), lambda i, c: (i, 0)),
                  pl.BlockSpec((1, d), lambda i, c: (0, 0)),
                  pl.BlockSpec((None, d, tf), lambda i, c: (layer, 0, c)),
                  pl.BlockSpec((None, tf, d), lambda i, c: (layer, c, 0)),
                  pl.BlockSpec((1, d), lambda i, c: (0, 0))],
        out_specs=pl.BlockSpec((tm, d), lambda i, c: (i, 0)),
        scratch_shapes=[pltpu.VMEM((tm, d), _BF16), pltpu.VMEM((tm, d), _F32)],
        compiler_params=pltpu.CompilerParams(
            dimension_semantics=("parallel", "arbitrary"),
            vmem_limit_bytes=_vmem_limit(bufs)),
        name=name,
    )(h, gpre, wup, wdn, gpost)


NA_QROWS = 4
NA_BAND = NA_QROWS + NA_KH


def _na_kernel(q_ref, k_ref, vt_ref, bfirst_ref, bmid_ref, blast_ref, o_ref, *, blocks_per_step, n_rows):
    step = pl.program_id(1)
    tq = NA_QROWS * GRID_W
    band = NA_BAND * GRID_W
    scs, vts = [], []
    for u in range(blocks_per_step):
        b = step * blocks_per_step + u
        row0 = jnp.clip(b * NA_QROWS - NA_KH // 2, 0, n_rows - NA_BAND)
        k0 = pl.multiple_of(row0 * GRID_W, NA_QROWS * GRID_W)
        sc = lax.dot_general(k_ref[pl.ds(k0, band), :], q_ref[u * tq:(u + 1) * tq, :], _NT,
                             preferred_element_type=_F32)
        bias = bmid_ref[0]
        if u == 0:
            bias = jnp.where(step == 0, bfirst_ref[0], bias)
        if u == blocks_per_step - 1:
            bias = jnp.where(step == n_rows // (NA_QROWS * blocks_per_step) - 1, blast_ref[0], bias)
        scs.append(sc + bias)
        vts.append(vt_ref[:, pl.ds(k0, band)])
    ps, ls = [], []
    for sc in scs:
        m = jnp.max(sc, axis=0, keepdims=True)
        p = jnp.exp2(sc - m)
        ls.append(jnp.sum(p, axis=0, keepdims=True))
        ps.append(p.astype(_BF16))
    for u in range(blocks_per_step):
        ot = jnp.dot(vts[u], ps[u], preferred_element_type=_F32) / ls[u]
        o_ref[u * tq:(u + 1) * tq, :] = ot.T.astype(_BF16)


def _na_bias_table(rpb, n_rows):
    col = np.arange(GRID_W)
    cs = np.clip(col - NA_KW // 2, 0, GRID_W - NA_KW)
    kc = np.arange(GRID_W)
    col_ok = (kc[:, None] >= cs[None, :]) & (kc[:, None] < cs[None, :] + NA_KW)
    coff = kc[:, None] - col[None, :] + (NA_KW - 1)
    n_ro, n_co = 2 * NA_KH - 1, 2 * NA_KW - 1
    onehot = ((np.arange(n_co)[:, None, None] == coff[None]) & col_ok[None]).astype(np.float32)
    cols = jnp.einsum("hrc,ckj->hrkj", rpb * _LOG2E, onehot, precision=lax.Precision.HIGHEST)
    cols = jnp.where(col_ok[None, None], cols, _MASKED)
    cols = jnp.pad(cols, ((0, 0), (NA_BAND, NA_BAND), (0, 0), (0, 0)), constant_values=_MASKED)
    cols = cols.reshape(NA_HEADS, (n_ro + 2 * NA_BAND) * GRID_W, GRID_W)
    tabs = []
    for i0 in (0, 2 * NA_QROWS, n_rows - NA_QROWS):
        row0 = int(np.clip(i0 - NA_KH // 2, 0, n_rows - NA_BAND))
        qi = i0 + np.arange(NA_QROWS)
        rs = np.clip(qi - NA_KH // 2, 0, n_rows - NA_KH)
        krow = row0 + np.arange(NA_BAND)
        row_ok = (krow[:, None] >= rs[None, :]) & (krow[:, None] < rs[None, :] + NA_KH)
        parts = []
        for qr, q in enumerate(qi):
            lo = (NA_BAND + row0 - int(q) + NA_KH - 1) * GRID_W
            ok = np.repeat(row_ok[:, qr], GRID_W)
            parts.append(jnp.where(ok[None, :, None], cols[:, lo:lo + NA_BAND * GRID_W], _MASKED))
        tabs.append(jnp.concatenate(parts, axis=2).astype(_F32))
    return tabs


def _na_attention(proj, vt, bias_tabs, *, blocks_per_step, name):
    s = proj.shape[0]
    n_rows = s // GRID_W
    assert blocks_per_step >= 2
    tq = blocks_per_step * NA_QROWS * GRID_W
    kcol = NA_W // HEAD_DIM
    bshape = (1, NA_BAND * GRID_W, NA_QROWS * GRID_W)
    bspecs = [pl.BlockSpec(bshape, lambda h, i: (h, 0, 0)) for _ in bias_tabs]
    bufs = [((s, HEAD_DIM), _BF16, 4), (bshape, _F32, 6), ((tq, HEAD_DIM), _BF16, 4),
            ((NA_BAND * GRID_W, NA_QROWS * GRID_W), _F32, 3 * blocks_per_step)]
    return pl.pallas_call(
        functools.partial(_na_kernel, blocks_per_step=blocks_per_step, n_rows=n_rows),
        out_shape=jax.ShapeDtypeStruct((s, NA_W), _BF16),
        grid=(NA_HEADS, n_rows // (NA_QROWS * blocks_per_step)),
        in_specs=[pl.BlockSpec((tq, HEAD_DIM), lambda h, i: (i, h)),
                  pl.BlockSpec((s, HEAD_DIM), lambda h, i: (0, kcol + h)),
                  pl.BlockSpec((HEAD_DIM, s), lambda h, i: (h, 0)),
                  *bspecs],
        out_specs=pl.BlockSpec((tq, HEAD_DIM), lambda h, i: (i, h)),
        compiler_params=pltpu.CompilerParams(
            dimension_semantics=("parallel", "arbitrary"),
            vmem_limit_bytes=_vmem_limit(bufs)),
        name=name,
    )(proj, proj, vt, *bias_tabs)


def _swa_kernel(sink_ref, q_ref, k_ref, vt_ref, b_ref, o_ref, *, tq, band, seq):
    n = pl.program_id(0)
    group = SW_Q_HEADS // SW_KV_HEADS
    start = pl.multiple_of(jnp.clip(n * tq - SW_WINDOW, 0, seq - band), SW_WINDOW)
    bias = b_ref[0]
    scs = []
    for kvh in range(SW_KV_HEADS):
        kb = k_ref[pl.ds(start, band), kvh * HEAD_DIM:(kvh + 1) * HEAD_DIM]
        q = jnp.concatenate([q_ref[:, (kvh * group + g) * HEAD_DIM:(kvh * group + g + 1) * HEAD_DIM]
                             for g in range(group)], axis=0)
        scs.append(lax.dot_general(kb, q, _NT, preferred_element_type=_F32))
    ps, denoms = [], []
    for kvh in range(SW_KV_HEADS):
        pg, dg = [], []
        for g in range(group):
            sink = sink_ref[kvh * group + g]
            sc = scs[kvh][:, g * tq:(g + 1) * tq] + bias
            m = jnp.maximum(jnp.max(sc, axis=0, keepdims=True), sink)
            p = jnp.exp2(sc - m)
            dg.append(jnp.sum(p, axis=0, keepdims=True) + jnp.exp2(sink - m))
            pg.append(p.astype(_BF16))
        ps.append(jnp.concatenate(pg, axis=1))
        denoms.append(jnp.concatenate(dg, axis=1))
    for kvh in range(SW_KV_HEADS):
        vt = vt_ref[kvh * HEAD_DIM:(kvh + 1) * HEAD_DIM, pl.ds(start, band)]
        ot = jnp.dot(vt, ps[kvh], preferred_element_type=_F32) / denoms[kvh]
        for g in range(group):
            hd = kvh * group + g
            o_ref[:, hd * HEAD_DIM:(hd + 1) * HEAD_DIM] = ot[:, g * tq:(g + 1) * tq].T.astype(_BF16)


def _swa_bias_table(tq, band):
    kr = np.arange(band)[:, None]
    qc = np.arange(tq)[None, :]
    tabs = [np.where(np.abs(qc - (rel + kr)) <= SW_WINDOW, 0.0, _MASKED)
            for rel in (0, -SW_WINDOW, tq - band)]
    return jnp.asarray(np.stack(tabs), _F32)


def _swa_attention(proj, vt, sinks_log2, *, tq, name):
    s = proj.shape[0]
    band = tq + 2 * SW_WINDOW
    n_blocks = s // tq
    qcol = 3 * NA_W // SW_QW
    kcol = (3 * NA_W + SW_QW) // SW_KVW
    rows = SW_Q_HEADS // SW_KV_HEADS * tq
    bufs = [((s, SW_KVW), _BF16, 4), ((tq, SW_QW), _BF16, 4), ((band, rows), _F32, 8)]
    variant = lambda n: jnp.where(n == 0, 0, jnp.where(n == n_blocks - 1, 2, 1))
    return pl.pallas_call(
        functools.partial(_swa_kernel, tq=tq, band=band, seq=s),
        out_shape=jax.ShapeDtypeStruct((s, SW_QW), _BF16),
        grid=(n_blocks,),
        in_specs=[pl.BlockSpec(memory_space=pltpu.SMEM),
                  pl.BlockSpec((tq, SW_QW), lambda n: (n, qcol)),
                  pl.BlockSpec((s, SW_KVW), lambda n: (0, kcol)),
                  pl.BlockSpec((SW_KVW, s), lambda n: (0, 0)),
                  pl.BlockSpec((1, band, tq), lambda n: (variant(n), 0, 0))],
        out_specs=pl.BlockSpec((tq, SW_QW), lambda n: (n, 0)),
        compiler_params=pltpu.CompilerParams(
            dimension_semantics=("arbitrary",),
            vmem_limit_bytes=_vmem_limit(bufs)),
        name=name,
    )(sinks_log2, proj, proj, vt, _swa_bias_table(tq, band))


def _diff_kernel(q_ref, k_ref, vt_ref, lq1_ref, lk1_ref, lq2_ref, lk2_ref, g_ref, o_ref,
                 m_ref, l_ref, acc_ref, s_ref, *, tk, seq, lambda_init):
    m_ref[...] = jnp.full(m_ref.shape, -jnp.inf, _F32)
    l_ref[...] = jnp.zeros(l_ref.shape, _F32)
    acc_ref[...] = jnp.zeros(acc_ref.shape, _F32)
    nblk = seq // tk

    def scores(kb, slot):
        k0 = pl.multiple_of(kb * tk, tk)
        for t in range(2):
            sl = slice(t * HEAD_DIM, (t + 1) * HEAD_DIM)
            s_ref[slot + t] = lax.dot_general(k_ref[pl.ds(k0, tk), sl], q_ref[:, sl], _NT,
                                              preferred_element_type=_F32)

    def update(kb, slot):
        k0 = pl.multiple_of(kb * tk, tk)
        vt = vt_ref[:, pl.ds(k0, tk)]
        for t in range(2):
            sc = s_ref[slot + t]
            m_prev = m_ref[t]
            m_new = jnp.maximum(m_prev, jnp.max(sc, axis=0, keepdims=True))
            alpha = jnp.exp2(m_prev - m_new)
            p = jnp.exp2(sc - m_new)
            l_ref[t] = alpha * l_ref[t] + jnp.sum(p, axis=0, keepdims=True)
            acc_ref[t] = alpha * acc_ref[t] + jnp.dot(vt, p.astype(_BF16), preferred_element_type=_F32)
            m_ref[t] = m_new

    scores(0, 0)

    def pair(i, carry):
        scores(2 * i + 1, 2)
        update(2 * i, 0)
        scores(2 * i + 2, 0)
        update(2 * i + 1, 2)
        return carry

    lax.fori_loop(0, nblk // 2 - 1, pair, 0)
    scores(nblk - 1, 2)
    update(nblk - 2, 0)
    update(nblk - 1, 2)

    lam = (jnp.exp(jnp.sum(lq1_ref[...] * lk1_ref[...], axis=-1, keepdims=True))
           - jnp.exp(jnp.sum(lq2_ref[...] * lk2_ref[...], axis=-1, keepdims=True)) + lambda_init)
    ot = acc_ref[0] / l_ref[0] - lam * (acc_ref[1] / l_ref[1])
    o_ref[...] = (_rms(ot.T, g_ref[...]) * (1.0 - lambda_init)).astype(_BF16)


def _diff_attention(proj, vt, lq1, lk1, lq2, lk2, subln_g, lambda_init, *, tq, tk, name):
    s = proj.shape[0]
    hw = 2 * HEAD_DIM
    kcol = DIFF_W // hw
    vec = pl.BlockSpec((1, HEAD_DIM), lambda h, qb: (0, 0))
    bufs = [((s, hw), _BF16, 4), ((tq, hw), _BF16, 4), ((tk, tq), _F32, 6),
            ((2, hw, tq), _F32, 2)]
    return pl.pallas_call(
        functools.partial(_diff_kernel, tk=tk, seq=s, lambda_init=lambda_init),
        out_shape=jax.ShapeDtypeStruct((s, DIFF_W), _BF16),
        grid=(DIFF_HEADS, s // tq),
        in_specs=[pl.BlockSpec((tq, hw), lambda h, qb: (qb, h)),
                  pl.BlockSpec((s, hw), lambda h, qb: (0, kcol + h)),
                  pl.BlockSpec((hw, s), lambda h, qb: (h, 0)),
                  vec, vec, vec, vec,
                  pl.BlockSpec((1, hw), lambda h, qb: (0, 0))],
        out_specs=pl.BlockSpec((tq, hw), lambda h, qb: (qb, h)),
        scratch_shapes=[pltpu.VMEM((2, 1, tq), _F32), pltpu.VMEM((2, 1, tq), _F32),
                        pltpu.VMEM((2, hw, tq), _F32), pltpu.VMEM((4, tk, tq), _F32)],
        compiler_params=pltpu.CompilerParams(
            dimension_semantics=("parallel", "arbitrary"),
            vmem_limit_bytes=_vmem_limit(bufs)),
        name=name,
    )(proj, proj, vt, lq1, lk1, lq2, lk2, subln_g)


def _rope_tables(seq):
    half = ROPE_DIM // 2
    inv = 1.0 / (ROPE_THETA ** (np.arange(0, ROPE_DIM, 2, dtype=np.float64) / ROPE_DIM))
    ang = np.arange(seq, dtype=np.float64)[:, None] * inv[None, :]
    cos, sin = np.cos(ang), np.sin(ang)
    pad = V7X_LANES - ROPE_DIM
    c = np.concatenate([cos, cos, np.ones((seq, pad))], axis=1)
    s1 = np.concatenate([-sin, np.zeros((seq, pad + half))], axis=1)
    s2 = np.concatenate([np.zeros((seq, half)), sin, np.zeros((seq, pad))], axis=1)
    return tuple(jnp.asarray(t, _F32) for t in (c, s1, s2))


def _col_scale(n, ranges):
    cs = np.ones((1, n), np.float32)
    for lo, hi in ranges:
        cs[:, lo:hi] = _QK_SCALE
    return jnp.asarray(cs)


def kernel(x, mem, even_w_in, even_w_out, na_rpb, sw_sinks, odd_w_in, odd_w_out, diff_lam_q1, diff_lam_k1, diff_lam_q2, diff_lam_k2, diff_subln_g, mix_pre_g, mix_post_g, mem_norm_g, mem_pre_g, mem_post_g, mem_wq, mem_wk, mem_wv, mem_wo, mlp_pre_g, mlp_post_g, mlp_w_up, mlp_w_down):
    b, s, d = x.shape
    assert (b, s, d) == (1, SEQ, D_MODEL)
    h = x.reshape(s, d)
    mem2 = mem.reshape(MEM_LEN, d)
    rope_tabs = _rope_tables(s)
    row = lambda v: v.reshape(1, -1)
    w_up, w_down = mlp_w_up.astype(_BF16), mlp_w_down.astype(_BF16)

    for layer in range(DEPTH):
        if layer % 2 == 0:
            e = layer // 2
            sw_q0 = 3 * NA_W
            proj = _norm_matmul(
                h, row(mix_pre_g[layer]), even_w_in[e],
                _col_scale(EVEN_IN, [(0, NA_W), (sw_q0, sw_q0 + SW_QW)]),
                rope_tabs, (sw_q0, sw_q0 + SW_QW + SW_KVW), tm=1024, tn=512, name=f"in_proj{layer}")
            out_a = _na_attention(proj, proj[:, 2 * NA_W:3 * NA_W].T, _na_bias_table(na_rpb[e], s // GRID_W),
                                  blocks_per_step=8, name=f"na{layer}")
            out_b = _swa_attention(proj, proj[:, EVEN_IN - SW_KVW:].T, sw_sinks[e] * _LOG2E,
                                   tq=256, name=f"swa{layer}")
            ys, w_out = [out_a, out_b], even_w_out[e].astype(_BF16)
        else:
            o = layer // 2
            lambda_init = 0.8 - 0.6 * math.exp(-0.3 * layer)
            proj = _norm_matmul(
                h, row(mix_pre_g[layer]), odd_w_in[o],
                _col_scale(ODD_IN, [(0, DIFF_W)]),
                rope_tabs, (0, 2 * DIFF_W), tm=1024, tn=512, name=f"in_proj{layer}")
            y = _diff_attention(proj, proj[:, 2 * DIFF_W:].T, row(diff_lam_q1[o]), row(diff_lam_k1[o]), row(diff_lam_q2[o]),
                                row(diff_lam_k2[o]), row(diff_subln_g[o]), lambda_init,
                                tq=1024, tk=1024, name=f"diff{layer}")
            ys, w_out = [y], odd_w_out[o].astype(_BF16)

        wkv = jnp.concatenate([mem_wk[layer], mem_wv[layer]], axis=1).astype(_BF16)
        kv = _norm_matmul(mem2, row(mem_norm_g[layer]), wkv, jnp.ones((1, 2 * MEM_W), _F32),
                          tm=MEM_LEN, tn=512, name=f"mem_kv{layer}")
        h = _post_mix(ys, w_out, h, row(mix_post_g[layer]), row(mem_pre_g[layer]),
                      mem_wq[layer].astype(_BF16), kv, mem_wo[layer].astype(_BF16),
                      row(mem_post_g[layer]), tm=512, name=f"post_mix{layer}")
        h = _mlp(h, row(mlp_pre_g[layer]), w_up, w_down, row(mlp_post_g[layer]), layer,
                 tm=512, tf=1024, name=f"mlp{layer}")
    return h.reshape(b, s, d)
```

```python
import functools
import math

import jax
import jax.numpy as jnp
import numpy as np
from jax import lax
from jax.experimental import pallas as pl
from jax.experimental.pallas import tpu as pltpu

D_MODEL = 2048
SEQ = 8192
DEPTH = 2
GRID_W = 64
HEAD_DIM = 128
ROPE_THETA = 500000.0
ROPE_DIM = HEAD_DIM // 4
NORM_EPS = 1e-6
NA_HEADS = 8
NA_KH = 8
NA_KW = 16
SW_Q_HEADS = 8
SW_KV_HEADS = 2
SW_WINDOW = 128
DIFF_HEADS = 8
MEM_LEN = 256
MEM_HEADS = 4
D_FF = 4 * D_MODEL
NA_W = NA_HEADS * HEAD_DIM
SW_QW = SW_Q_HEADS * HEAD_DIM
SW_KVW = SW_KV_HEADS * HEAD_DIM
EVEN_IN = 3 * NA_W + SW_QW + 2 * SW_KVW
DIFF_W = DIFF_HEADS * 2 * HEAD_DIM
ODD_IN = 3 * DIFF_W
MEM_W = MEM_HEADS * HEAD_DIM

V7X_LANES = 128
V7X_VMEM_BYTES = 64 * 1024 * 1024

_F32 = jnp.float32
_BF16 = jnp.bfloat16
_LOG2E = 1.4426950408889634
_QK_SCALE = HEAD_DIM ** -0.5 * _LOG2E
_MASKED = -1e30
_NT = (((1,), (1,)), ((), ()))


def _nbytes(shape, dtype):
    return math.prod(shape) * jnp.dtype(dtype).itemsize


def _vmem_limit(buffers):
    need = sum(_nbytes(s, d) * n for s, d, n in buffers)
    return min(V7X_VMEM_BYTES - (4 << 20), need + (8 << 20))


def _rms(x, g):
    ms = jnp.mean(x * x, axis=-1, keepdims=True)
    return x * lax.rsqrt(ms + NORM_EPS) * g


def _rope_rot(a, c, s1, s2):
    return (a * c + pltpu.roll(a, V7X_LANES - ROPE_DIM // 2, 1) * s1
            + pltpu.roll(a, ROPE_DIM // 2, 1) * s2)


def _norm_matmul_kernel(x_ref, g_ref, w_ref, cs_ref, *rest, nchunk, mrows, rope_lo, rope_hi):
    if rope_lo is None:
        o_ref, hn_ref = rest
    else:
        c_ref, s1_ref, s2_ref, o_ref, hn_ref = rest
    j = pl.program_id(1)

    @pl.when(j == 0)
    def _():
        hn_ref[...] = _rms(x_ref[...], g_ref[...]).astype(_BF16)

    w = w_ref[...].astype(_BF16)
    cs = cs_ref[...]
    for r in range(x_ref.shape[0] // mrows):
        rows = slice(r * mrows, (r + 1) * mrows)
        acc = jnp.dot(hn_ref[rows, :], w, preferred_element_type=_F32) * cs
        for c in range(nchunk):
            sl = slice(c * V7X_LANES, (c + 1) * V7X_LANES)
            a = acc[:, sl]
            if rope_lo is not None:
                gc = j * nchunk + c
                is_rope = jnp.logical_and(gc >= rope_lo, gc < rope_hi)
                a = _rope_rot(a, jnp.where(is_rope, c_ref[rows, :], 1.0),
                              jnp.where(is_rope, s1_ref[rows, :], 0.0),
                              jnp.where(is_rope, s2_ref[rows, :], 0.0))
            o_ref[rows, sl] = a.astype(_BF16)


def _norm_matmul(x, g, w, colscale, rope_tabs=None, rope_cols=None, *, tm, tn, name):
    s, d = x.shape
    n = w.shape[1]
    nchunk = tn // V7X_LANES
    in_specs = [
        pl.BlockSpec((tm, d), lambda i, j: (i, 0)),
        pl.BlockSpec((1, d), lambda i, j: (0, 0)),
        pl.BlockSpec((d, tn), lambda i, j: (0, j)),
        pl.BlockSpec((1, tn), lambda i, j: (0, j)),
    ]
    args = [x, g, w, colscale]
    rope_lo = rope_hi = None
    if rope_cols is not None:
        rope_lo, rope_hi = rope_cols[0] // V7X_LANES, rope_cols[1] // V7X_LANES
        in_specs += [pl.BlockSpec((tm, V7X_LANES), lambda i, j: (i, 0))] * 3
        args += list(rope_tabs)
    bufs = [((tm, d), _F32, 3), ((d, tn), w.dtype, 2), ((d, tn), _BF16, 1), ((tm, tn), _F32, 2),
            ((tm, d), _BF16, 1)]
    return pl.pallas_call(
        functools.partial(_norm_matmul_kernel, nchunk=nchunk, mrows=min(tm, 128), rope_lo=rope_lo,
                          rope_hi=rope_hi),
        out_shape=jax.ShapeDtypeStruct((s, n), _BF16),
        grid=(s // tm, n // tn),
        in_specs=in_specs,
        out_specs=pl.BlockSpec((tm, tn), lambda i, j: (i, j)),
        scratch_shapes=[pltpu.VMEM((tm, d), _BF16)],
        compiler_params=pltpu.CompilerParams(
            dimension_semantics=("parallel", "arbitrary"),
            vmem_limit_bytes=_vmem_limit(bufs)),
        name=name,
    )(*args)


def _post_mix_kernel(*refs, nparts):
    ys = refs[:nparts]
    wout_ref, h_ref, gmix_ref, gpre_ref, wq_ref, kv_ref, wo_ref, gpost_ref, o_ref = refs[nparts:]
    acc, k0 = None, 0
    for y_ref in ys:
        kk = y_ref.shape[1]
        part = jnp.dot(y_ref[...], wout_ref[k0:k0 + kk, :], preferred_element_type=_F32)
        acc = part if acc is None else acc + part
        k0 += kk
    x = h_ref[...] + _rms(acc, gmix_ref[...])
    hn = _rms(x, gpre_ref[...]).astype(_BF16)
    q = (jnp.dot(hn, wq_ref[...], preferred_element_type=_F32) * _QK_SCALE).astype(_BF16)
    outs = []
    for hd in range(MEM_HEADS):
        sl = slice(hd * HEAD_DIM, (hd + 1) * HEAD_DIM)
        k = kv_ref[:, sl]
        v = kv_ref[:, MEM_W + hd * HEAD_DIM:MEM_W + (hd + 1) * HEAD_DIM]
        sc = lax.dot_general(q[:, sl], k, _NT, preferred_element_type=_F32)
        m = jnp.max(sc, axis=-1, keepdims=True)
        p = jnp.exp2(sc - m)
        l = jnp.sum(p, axis=-1, keepdims=True)
        o = jnp.dot(p.astype(_BF16), v, preferred_element_type=_F32) / l
        outs.append(o.astype(_BF16))
    o_all = jnp.concatenate(outs, axis=-1)
    c = jnp.dot(o_all, wo_ref[...], preferred_element_type=_F32)
    o_ref[...] = x + _rms(c, gpost_ref[...])


def _post_mix(ys, wout, h, gmix, gpre, wq, kv, wo, gpost, *, tm, name):
    s, d = h.shape
    nparts = len(ys)
    row = lambda i: (i, 0)
    fixed = lambda i: (0, 0)
    whole = lambda a: pl.BlockSpec(a.shape, fixed, pipeline_mode=pl.Buffered(1))
    in_specs = [pl.BlockSpec((tm, y.shape[1]), row) for y in ys]
    in_specs += [whole(wout), pl.BlockSpec((tm, d), row), whole(gmix), whole(gpre), whole(wq), whole(kv),
                 whole(wo), whole(gpost)]
    bufs = [((tm, y.shape[1]), _BF16, 2) for y in ys]
    bufs += [(wout.shape, _BF16, 1), (wq.shape, _BF16, 1), (wo.shape, _BF16, 1), (kv.shape, _BF16, 1),
             ((tm, d), _F32, 8), ((tm, d), _BF16, 1), ((tm, MEM_W), _F32, 4)]
    return pl.pallas_call(
        functools.partial(_post_mix_kernel, nparts=nparts),
        out_shape=jax.ShapeDtypeStruct((s, d), _F32),
        grid=(s // tm,),
        in_specs=in_specs,
        out_specs=pl.BlockSpec((tm, d), row),
        compiler_params=pltpu.CompilerParams(
            dimension_semantics=("parallel",),
            vmem_limit_bytes=_vmem_limit(bufs)),
        name=name,
    )(*ys, wout, h, gmix, gpre, wq, kv, wo, gpost)


def _mlp_kernel(h_ref, gpre_ref, wup_ref, wdn_ref, gpost_ref, o_ref, hn_ref, acc_ref):
    c = pl.program_id(1)

    @pl.when(c == 0)
    def _():
        hn_ref[...] = _rms(h_ref[...], gpre_ref[...]).astype(_BF16)
        acc_ref[...] = jnp.zeros(acc_ref.shape, _F32)

    u = jnp.maximum(jnp.dot(hn_ref[...], wup_ref[...], preferred_element_type=_F32), 0.0)
    acc_ref[...] += jnp.dot((u * u).astype(_BF16), wdn_ref[...], preferred_element_type=_F32)

    @pl.when(c == pl.num_programs(1) - 1)
    def _():
        o_ref[...] = h_ref[...] + _rms(acc_ref[...], gpost_ref[...])


def _mlp(h, gpre, wup, wdn, gpost, layer, *, tm, tf, name):
    s, d = h.shape
    f = wup.shape[2]
    bufs = [((tm, d), _F32, 5), ((tm, d), _BF16, 1), ((d, tf), _BF16, 4), ((tm, tf), _F32, 2)]
    return pl.pallas_call(
        _mlp_kernel,
        out_shape=jax.ShapeDtypeStruct((s, d), _F32),
        grid=(s // tm, f // tf),
        in_specs=[pl.BlockSpec((tm, d), lambda i, c: (i, 0)),
                  pl.BlockSpec((1, d), lambda i, c: (0, 0)),
                  pl.BlockSpec((None, d, tf), lambda i, c: (layer, 0, c)),
                  pl.BlockSpec((None, tf, d), lambda i, c: (layer, c, 0)),
                  pl.BlockSpec((1, d), lambda i, c: (0, 0))],
        out_specs=pl.BlockSpec((tm, d), lambda i, c: (i, 0)),
        scratch_shapes=[pltpu.VMEM((tm, d), _BF16), pltpu.VMEM((tm, d), _F32)],
        compiler_params=pltpu.CompilerParams(
            dimension_semantics=("parallel", "arbitrary"),
            vmem_limit_bytes=_vmem_limit(bufs)),
        name=name,
    )(h, gpre, wup, wdn, gpost)


NA_QROWS = 4
NA_BAND = NA_QROWS + NA_KH


def _na_kernel(q_ref, k_ref, vt_ref, bfirst_ref, bmid_ref, blast_ref, o_ref, *, blocks_per_step, n_rows):
    step = pl.program_id(1)
    tq = NA_QROWS * GRID_W
    band = NA_BAND * GRID_W
    scs, vts = [], []
    for u in range(blocks_per_step):
        b = step * blocks_per_step + u
        row0 = jnp.clip(b * NA_QROWS - NA_KH // 2, 0, n_rows - NA_BAND)
        k0 = pl.multiple_of(row0 * GRID_W, NA_QROWS * GRID_W)
        sc = lax.dot_general(k_ref[pl.ds(k0, band), :], q_ref[u * tq:(u + 1) * tq, :], _NT,
                             preferred_element_type=_F32)
        bias = bmid_ref[0]
        if u == 0:
            bias = jnp.where(step == 0, bfirst_ref[0], bias)
        if u == blocks_per_step - 1:
            bias = jnp.where(step == n_rows // (NA_QROWS * blocks_per_step) - 1, blast_ref[0], bias)
        scs.append(sc + bias)
        vts.append(vt_ref[:, pl.ds(k0, band)])
    ps, ls = [], []
    for sc in scs:
        m = jnp.max(sc, axis=0, keepdims=True)
        p = jnp.exp2(sc - m)
        ls.append(jnp.sum(p, axis=0, keepdims=True))
        ps.append(p.astype(_BF16))
    for u in range(blocks_per_step):
        ot = jnp.dot(vts[u], ps[u], preferred_element_type=_F32) / ls[u]
        o_ref[u * tq:(u + 1) * tq, :] = ot.T.astype(_BF16)


def _na_bias_table(rpb, n_rows):
    col = np.arange(GRID_W)
    cs = np.clip(col - NA_KW // 2, 0, GRID_W - NA_KW)
    kc = np.arange(GRID_W)
    col_ok = (kc[:, None] >= cs[None, :]) & (kc[:, None] < cs[None, :] + NA_KW)
    coff = kc[:, None] - col[None, :] + (NA_KW - 1)
    n_ro, n_co = 2 * NA_KH - 1, 2 * NA_KW - 1
    onehot = ((np.arange(n_co)[:, None, None] == coff[None]) & col_ok[None]).astype(np.float32)
    cols = jnp.einsum("hrc,ckj->hrkj", rpb * _LOG2E, onehot, precision=lax.Precision.HIGHEST)
    cols = jnp.where(col_ok[None, None], cols, _MASKED)
    cols = jnp.pad(cols, ((0, 0), (NA_BAND, NA_BAND), (0, 0), (0, 0)), constant_values=_MASKED)
    cols = cols.reshape(NA_HEADS, (n_ro + 2 * NA_BAND) * GRID_W, GRID_W)
    tabs = []
    for i0 in (0, 2 * NA_QROWS, n_rows - NA_QROWS):
        row0 = int(np.clip(i0 - NA_KH // 2, 0, n_rows - NA_BAND))
        qi = i0 + np.arange(NA_QROWS)
        rs = np.clip(qi - NA_KH // 2, 0, n_rows - NA_KH)
        krow = row0 + np.arange(NA_BAND)
        row_ok = (krow[:, None] >= rs[None, :]) & (krow[:, None] < rs[None, :] + NA_KH)
        parts = []
        for qr, q in enumerate(qi):
            lo = (NA_BAND + row0 - int(q) + NA_KH - 1) * GRID_W
            ok = np.repeat(row_ok[:, qr], GRID_W)
            parts.append(jnp.where(ok[None, :, None], cols[:, lo:lo + NA_BAND * GRID_W], _MASKED))
        tabs.append(jnp.concatenate(parts, axis=2).astype(_F32))
    return tabs


def _na_attention(proj, vt, bias_tabs, *, blocks_per_step, name):
    s = proj.shape[0]
    n_rows = s // GRID_W
    assert blocks_per_step >= 2
    tq = blocks_per_step * NA_QROWS * GRID_W
    kcol = NA_W // HEAD_DIM
    bshape = (1, NA_BAND * GRID_W, NA_QROWS * GRID_W)
    bspecs = [pl.BlockSpec(bshape, lambda h, i: (h, 0, 0)) for _ in bias_tabs]
    bufs = [((s, HEAD_DIM), _BF16, 4), (bshape, _F32, 6), ((tq, HEAD_DIM), _BF16, 4),
            ((NA_BAND * GRID_W, NA_QROWS * GRID_W), _F32, 3 * blocks_per_step)]
    return pl.pallas_call(
        functools.partial(_na_kernel, blocks_per_step=blocks_per_step, n_rows=n_rows),
        out_shape=jax.ShapeDtypeStruct((s, NA_W), _BF16),
        grid=(NA_HEADS, n_rows // (NA_QROWS * blocks_per_step)),
        in_specs=[pl.BlockSpec((tq, HEAD_DIM), lambda h, i: (i, h)),
                  pl.BlockSpec((s, HEAD_DIM), lambda h, i: (0, kcol + h)),
                  pl.BlockSpec((HEAD_DIM, s), lambda h, i: (h, 0)),
                  *bspecs],
        out_specs=pl.BlockSpec((tq, HEAD_DIM), lambda h, i: (i, h)),
        compiler_params=pltpu.CompilerParams(
            dimension_semantics=("parallel", "arbitrary"),
            vmem_limit_bytes=_vmem_limit(bufs)),
        name=name,
    )(proj, proj, vt, *bias_tabs)


def _swa_kernel(sink_ref, q_ref, k_ref, vt_ref, b_ref, o_ref, *, tq, band, seq):
    n = pl.program_id(0)
    group = SW_Q_HEADS // SW_KV_HEADS
    start = pl.multiple_of(jnp.clip(n * tq - SW_WINDOW, 0, seq - band), SW_WINDOW)
    bias = b_ref[0]
    scs = []
    for kvh in range(SW_KV_HEADS):
        kb = k_ref[pl.ds(start, band), kvh * HEAD_DIM:(kvh + 1) * HEAD_DIM]
        q = jnp.concatenate([q_ref[:, (kvh * group + g) * HEAD_DIM:(kvh * group + g + 1) * HEAD_DIM]
                             for g in range(group)], axis=0)
        scs.append(lax.dot_general(kb, q, _NT, preferred_element_type=_F32))
    ps, denoms = [], []
    for kvh in range(SW_KV_HEADS):
        pg, dg = [], []
        for g in range(group):
            sink = sink_ref[kvh * group + g]
            sc = scs[kvh][:, g * tq:(g + 1) * tq] + bias
            m = jnp.maximum(jnp.max(sc, axis=0, keepdims=True), sink)
            p = jnp.exp2(sc - m)
            dg.append(jnp.sum(p, axis=0, keepdims=True) + jnp.exp2(sink - m))
            pg.append(p.astype(_BF16))
        ps.append(jnp.concatenate(pg, axis=1))
        denoms.append(jnp.concatenate(dg, axis=1))
    for kvh in range(SW_KV_HEADS):
        vt = vt_ref[kvh * HEAD_DIM:(kvh + 1) * HEAD_DIM, pl.ds(start, band)]
        ot = jnp.dot(vt, ps[kvh], preferred_element_type=_F32) / denoms[kvh]
        for g in range(group):
            hd = kvh * group + g
            o_ref[:, hd * HEAD_DIM:(hd + 1) * HEAD_DIM] = ot[:, g * tq:(g + 1) * tq].T.astype(_BF16)


def _swa_bias_table(tq, band):
    kr = np.arange(band)[:, None]
    qc = np.arange(tq)[None, :]
    tabs = [np.where(np.abs(qc - (rel + kr)) <= SW_WINDOW, 0.0, _MASKED)
            for rel in (0, -SW_WINDOW, tq - band)]
    return jnp.asarray(np.stack(tabs), _F32)


def _swa_attention(proj, vt, sinks_log2, *, tq, name):
    s = proj.shape[0]
    band = tq + 2 * SW_WINDOW
    n_blocks = s // tq
    qcol = 3 * NA_W // SW_QW
    kcol = (3 * NA_W + SW_QW) // SW_KVW
    rows = SW_Q_HEADS // SW_KV_HEADS * tq
    bufs = [((s, SW_KVW), _BF16, 4), ((tq, SW_QW), _BF16, 4), ((band, rows), _F32, 8)]
    variant = lambda n: jnp.where(n == 0, 0, jnp.where(n == n_blocks - 1, 2, 1))
    return pl.pallas_call(
        functools.partial(_swa_kernel, tq=tq, band=band, seq=s),
        out_shape=jax.ShapeDtypeStruct((s, SW_QW), _BF16),
        grid=(n_blocks,),
        in_specs=[pl.BlockSpec(memory_space=pltpu.SMEM),
                  pl.BlockSpec((tq, SW_QW), lambda n: (n, qcol)),
                  pl.BlockSpec((s, SW_KVW), lambda n: (0, kcol)),
                  pl.BlockSpec((SW_KVW, s), lambda n: (0, 0)),
                  pl.BlockSpec((1, band, tq), lambda n: (variant(n), 0, 0))],
        out_specs=pl.BlockSpec((tq, SW_QW), lambda n: (n, 0)),
        compiler_params=pltpu.CompilerParams(
            dimension_semantics=("arbitrary",),
            vmem_limit_bytes=_vmem_limit(bufs)),
        name=name,
    )(sinks_log2, proj, proj, vt, _swa_bias_table(tq, band))


def _diff_kernel(q_ref, k_ref, vt_ref, lq1_ref, lk1_ref, lq2_ref, lk2_ref, g_ref, o_ref,
                 m_ref, l_ref, acc_ref, s_ref, *, tk, seq, lambda_init):
    m_ref[...] = jnp.full(m_ref.shape, -jnp.inf, _F32)
    l_ref[...] = jnp.zeros(l_ref.shape, _F32)
    acc_ref[...] = jnp.zeros(acc_ref.shape, _F32)
    nblk = seq // tk

    def scores(kb, slot):
        k0 = pl.multiple_of(kb * tk, tk)
        for t in range(2):
            sl = slice(t * HEAD_DIM, (t + 1) * HEAD_DIM)
            s_ref[slot + t] = lax.dot_general(k_ref[pl.ds(k0, tk), sl], q_ref[:, sl], _NT,
                                              preferred_element_type=_F32)

    def update(kb, slot):
        k0 = pl.multiple_of(kb * tk, tk)
        vt = vt_ref[:, pl.ds(k0, tk)]
        for t in range(2):
            sc = s_ref[slot + t]
            m_prev = m_ref[t]
            m_new = jnp.maximum(m_prev, jnp.max(sc, axis=0, keepdims=True))
            alpha = jnp.exp2(m_prev - m_new)
            p = jnp.exp2(sc - m_new)
            l_ref[t] = alpha * l_ref[t] + jnp.sum(p, axis=0, keepdims=True)
            acc_ref[t] = alpha * acc_ref[t] + jnp.dot(vt, p.astype(_BF16), preferred_element_type=_F32)
            m_ref[t] = m_new

    scores(0, 0)

    def pair(i, carry):
        scores(2 * i + 1, 2)
        update(2 * i, 0)
        scores(2 * i + 2, 0)
        update(2 * i + 1, 2)
        return carry

    lax.fori_loop(0, nblk // 2 - 1, pair, 0)
    scores(nblk - 1, 2)
    update(nblk - 2, 0)
    update(nblk - 1, 2)

    lam = (jnp.exp(jnp.sum(lq1_ref[...] * lk1_ref[...], axis=-1, keepdims=True))
           - jnp.exp(jnp.sum(lq2_ref[...] * lk2_ref[...], axis=-1, keepdims=True)) + lambda_init)
    ot = acc_ref[0] / l_ref[0] - lam * (acc_ref[1] / l_ref[1])
    o_ref[...] = (_rms(ot.T, g_ref[...]) * (1.0 - lambda_init)).astype(_BF16)


def _diff_attention(proj, vt, lq1, lk1, lq2, lk2, subln_g, lambda_init, *, tq, tk, name):
    s = proj.shape[0]
    hw = 2 * HEAD_DIM
    kcol = DIFF_W // hw
    vec = pl.BlockSpec((1, HEAD_DIM), lambda h, qb: (0, 0))
    bufs = [((s, hw), _BF16, 4), ((tq, hw), _BF16, 4), ((tk, tq), _F32, 6),
            ((2, hw, tq), _F32, 2)]
    return pl.pallas_call(
        functools.partial(_diff_kernel, tk=tk, seq=s, lambda_init=lambda_init),
        out_shape=jax.ShapeDtypeStruct((s, DIFF_W), _BF16),
        grid=(DIFF_HEADS, s // tq),
        in_specs=[pl.BlockSpec((tq, hw), lambda h, qb: (qb, h)),
                  pl.BlockSpec((s, hw), lambda h, qb: (0, kcol + h)),
                  pl.BlockSpec((hw, s), lambda h, qb: (h, 0)),
                  vec, vec, vec, vec,
                  pl.BlockSpec((1, hw), lambda h, qb: (0, 0))],
        out_specs=pl.BlockSpec((tq, hw), lambda h, qb: (qb, h)),
        scratch_shapes=[pltpu.VMEM((2, 1, tq), _F32), pltpu.VMEM((2, 1, tq), _F32),
                        pltpu.VMEM((2, hw, tq), _F32), pltpu.VMEM((4, tk, tq), _F32)],
        compiler_params=pltpu.CompilerParams(
            dimension_semantics=("parallel", "arbitrary"),
            vmem_limit_bytes=_vmem_limit(bufs)),
        name=name,
    )(proj, proj, vt, lq1, lk1, lq2, lk2, subln_g)


def _rope_tables(seq):
    half = ROPE_DIM // 2
    inv = 1.0 / (ROPE_THETA ** (np.arange(0, ROPE_DIM, 2, dtype=np.float64) / ROPE_DIM))
    ang = np.arange(seq, dtype=np.float64)[:, None] * inv[None, :]
    cos, sin = np.cos(ang), np.sin(ang)
    pad = V7X_LANES - ROPE_DIM
    c = np.concatenate([cos, cos, np.ones((seq, pad))], axis=1)
    s1 = np.concatenate([-sin, np.zeros((seq, pad + half))], axis=1)
    s2 = np.concatenate([np.zeros((seq, half)), sin, np.zeros((seq, pad))], axis=1)
    return tuple(jnp.asarray(t, _F32) for t in (c, s1, s2))


def _col_scale(n, ranges):
    cs = np.ones((1, n), np.float32)
    for lo, hi in ranges:
        cs[:, lo:hi] = _QK_SCALE
    return jnp.asarray(cs)


def kernel(x, mem, even_w_in, even_w_out, na_rpb, sw_sinks, odd_w_in, odd_w_out, diff_lam_q1, diff_lam_k1, diff_lam_q2, diff_lam_k2, diff_subln_g, mix_pre_g, mix_post_g, mem_norm_g, mem_pre_g, mem_post_g, mem_wq, mem_wk, mem_wv, mem_wo, mlp_pre_g, mlp_post_g, mlp_w_up, mlp_w_down):
    b, s, d = x.shape
    assert (b, s, d) == (1, SEQ, D_MODEL)
    h = x.reshape(s, d)
    mem2 = mem.reshape(MEM_LEN, d)
    rope_tabs = _rope_tables(s)
    row = lambda v: v.reshape(1, -1)
    w_up, w_down = mlp_w_up.astype(_BF16), mlp_w_down.astype(_BF16)

    for layer in range(DEPTH):
        if layer % 2 == 0:
            e = layer // 2
            sw_q0 = 3 * NA_W
            proj = _norm_matmul(
                h, row(mix_pre_g[layer]), even_w_in[e],
                _col_scale(EVEN_IN, [(0, NA_W), (sw_q0, sw_q0 + SW_QW)]),
                rope_tabs, (sw_q0, sw_q0 + SW_QW + SW_KVW), tm=1024, tn=512, name=f"in_proj{layer}")
            out_a = _na_attention(proj, proj[:, 2 * NA_W:3 * NA_W].T, _na_bias_table(na_rpb[e], s // GRID_W),
                                  blocks_per_step=16, name=f"na{layer}")
            out_b = _swa_attention(proj, proj[:, EVEN_IN - SW_KVW:].T, sw_sinks[e] * _LOG2E,
                                   tq=256, name=f"swa{layer}")
            ys, w_out = [out_a, out_b], even_w_out[e].astype(_BF16)
        else:
            o = layer // 2
            lambda_init = 0.8 - 0.6 * math.exp(-0.3 * layer)
            proj = _norm_matmul(
                h, row(mix_pre_g[layer]), odd_w_in[o],
                _col_scale(ODD_IN, [(0, DIFF_W)]),
                rope_tabs, (0, 2 * DIFF_W), tm=1024, tn=512, name=f"in_proj{layer}")
            y = _diff_attention(proj, proj[:, 2 * DIFF_W:].T, row(diff_lam_q1[o]), row(diff_lam_k1[o]), row(diff_lam_q2[o]),
                                row(diff_lam_k2[o]), row(diff_subln_g[o]), lambda_init,
                                tq=1024, tk=1024, name=f"diff{layer}")
            ys, w_out = [y], odd_w_out[o].astype(_BF16)

        wkv = jnp.concatenate([mem_wk[layer], mem_wv[layer]], axis=1).astype(_BF16)
        kv = _norm_matmul(mem2, row(mem_norm_g[layer]), wkv, jnp.ones((1, 2 * MEM_W), _F32),
                          tm=MEM_LEN, tn=512, name=f"mem_kv{layer}")
        h = _post_mix(ys, w_out, h, row(mix_post_g[layer]), row(mem_pre_g[layer]),
                      mem_wq[layer].astype(_BF16), kv, mem_wo[layer].astype(_BF16),
                      row(mem_post_g[layer]), tm=512, name=f"post_mix{layer}")
        h = _mlp(h, row(mlp_pre_g[layer]), w_up, w_down, row(mlp_post_g[layer]), layer,
                 tm=512, tf=1024, name=f"mlp{layer}")
    return h.reshape(b, s, d)
```

```python
import functools
import math

import jax
import jax.numpy as jnp
import numpy as np
from jax import lax
from jax.experimental import pallas as pl
from jax.experimental.pallas import tpu as pltpu

D_MODEL = 2048
SEQ = 8192
DEPTH = 2
GRID_W = 64
HEAD_DIM = 128
ROPE_THETA = 500000.0
ROPE_DIM = HEAD_DIM // 4
NORM_EPS = 1e-6
NA_HEADS = 8
NA_KH = 8
NA_KW = 16
SW_Q_HEADS = 8
SW_KV_HEADS = 2
SW_WINDOW = 128
DIFF_HEADS = 8
MEM_LEN = 256
MEM_HEADS = 4
D_FF = 4 * D_MODEL
NA_W = NA_HEADS * HEAD_DIM
SW_QW = SW_Q_HEADS * HEAD_DIM
SW_KVW = SW_KV_HEADS * HEAD_DIM
EVEN_IN = 3 * NA_W + SW_QW + 2 * SW_KVW
DIFF_W = DIFF_HEADS * 2 * HEAD_DIM
ODD_IN = 3 * DIFF_W
MEM_W = MEM_HEADS * HEAD_DIM

V7X_LANES = 128
V7X_VMEM_BYTES = 64 * 1024 * 1024

_F32 = jnp.float32
_BF16 = jnp.bfloat16
_LOG2E = 1.4426950408889634
_QK_SCALE = HEAD_DIM ** -0.5 * _LOG2E
_MASKED = -1e30
_NT = (((1,), (1,)), ((), ()))
_ONES_ROWS = 16


def _nbytes(shape, dtype):
    return math.prod(shape) * jnp.dtype(dtype).itemsize


def _vmem_limit(buffers):
    need = sum(_nbytes(s, d) * n for s, d, n in buffers)
    return min(V7X_VMEM_BYTES - (4 << 20), need + (8 << 20))


def _rms(x, g):
    ms = jnp.mean(x * x, axis=-1, keepdims=True)
    return x * lax.rsqrt(ms + NORM_EPS) * g


def _rope_rot(a, c, s1, s2):
    return (a * c + pltpu.roll(a, V7X_LANES - ROPE_DIM // 2, 1) * s1
            + pltpu.roll(a, ROPE_DIM // 2, 1) * s2)


def _norm_matmul_kernel(x_ref, g_ref, w_ref, cs_ref, *rest, nchunk, mrows, rope_lo, rope_hi):
    if rope_lo is None:
        o_ref, hn_ref = rest
    else:
        c_ref, s1_ref, s2_ref, o_ref, hn_ref = rest
    j = pl.program_id(1)

    @pl.when(j == 0)
    def _():
        hn_ref[...] = _rms(x_ref[...], g_ref[...]).astype(_BF16)

    w = w_ref[...].astype(_BF16)
    cs = cs_ref[...]
    for r in range(x_ref.shape[0] // mrows):
        rows = slice(r * mrows, (r + 1) * mrows)
        acc = jnp.dot(hn_ref[rows, :], w, preferred_element_type=_F32) * cs
        for c in range(nchunk):
            sl = slice(c * V7X_LANES, (c + 1) * V7X_LANES)
            a = acc[:, sl]
            if rope_lo is not None:
                gc = j * nchunk + c
                is_rope = jnp.logical_and(gc >= rope_lo, gc < rope_hi)
                a = _rope_rot(a, jnp.where(is_rope, c_ref[rows, :], 1.0),
                              jnp.where(is_rope, s1_ref[rows, :], 0.0),
                              jnp.where(is_rope, s2_ref[rows, :], 0.0))
            o_ref[rows, sl] = a.astype(_BF16)


def _norm_matmul(x, g, w, colscale, rope_tabs=None, rope_cols=None, *, tm, tn, name):
    s, d = x.shape
    n = w.shape[1]
    nchunk = tn // V7X_LANES
    in_specs = [
        pl.BlockSpec((tm, d), lambda i, j: (i, 0)),
        pl.BlockSpec((1, d), lambda i, j: (0, 0)),
        pl.BlockSpec((d, tn), lambda i, j: (0, j)),
        pl.BlockSpec((1, tn), lambda i, j: (0, j)),
    ]
    args = [x, g, w, colscale]
    rope_lo = rope_hi = None
    if rope_cols is not None:
        rope_lo, rope_hi = rope_cols[0] // V7X_LANES, rope_cols[1] // V7X_LANES
        in_specs += [pl.BlockSpec((tm, V7X_LANES), lambda i, j: (i, 0))] * 3
        args += list(rope_tabs)
    bufs = [((tm, d), _F32, 3), ((d, tn), w.dtype, 2), ((d, tn), _BF16, 1), ((tm, tn), _F32, 2),
            ((tm, d), _BF16, 1)]
    return pl.pallas_call(
        functools.partial(_norm_matmul_kernel, nchunk=nchunk, mrows=min(tm, 128), rope_lo=rope_lo,
                          rope_hi=rope_hi),
        out_shape=jax.ShapeDtypeStruct((s, n), _BF16),
        grid=(s // tm, n // tn),
        in_specs=in_specs,
        out_specs=pl.BlockSpec((tm, tn), lambda i, j: (i, j)),
        scratch_shapes=[pltpu.VMEM((tm, d), _BF16)],
        compiler_params=pltpu.CompilerParams(
            dimension_semantics=("parallel", "arbitrary"),
            vmem_limit_bytes=_vmem_limit(bufs)),
        name=name,
    )(*args)


def _post_mix_kernel(*refs, nparts):
    ys = refs[:nparts]
    wout_ref, h_ref, gmix_ref, gpre_ref, wq_ref, kv_ref, wo_ref, gpost_ref, o_ref = refs[nparts:]
    acc, k0 = None, 0
    for y_ref in ys:
        kk = y_ref.shape[1]
        part = jnp.dot(y_ref[...], wout_ref[k0:k0 + kk, :], preferred_element_type=_F32)
        acc = part if acc is None else acc + part
        k0 += kk
    x = h_ref[...] + _rms(acc, gmix_ref[...])
    hn = _rms(x, gpre_ref[...]).astype(_BF16)
    q = (jnp.dot(hn, wq_ref[...], preferred_element_type=_F32) * _QK_SCALE).astype(_BF16)
    outs = []
    for hd in range(MEM_HEADS):
        sl = slice(hd * HEAD_DIM, (hd + 1) * HEAD_DIM)
        k = kv_ref[:, sl]
        v = kv_ref[:, MEM_W + hd * HEAD_DIM:MEM_W + (hd + 1) * HEAD_DIM]
        sc = lax.dot_general(q[:, sl], k, _NT, preferred_element_type=_F32)
        m = jnp.max(sc, axis=-1, keepdims=True)
        p = jnp.exp2(sc - m)
        l = jnp.sum(p, axis=-1, keepdims=True)
        o = jnp.dot(p.astype(_BF16), v, preferred_element_type=_F32) / l
        outs.append(o.astype(_BF16))
    o_all = jnp.concatenate(outs, axis=-1)
    c = jnp.dot(o_all, wo_ref[...], preferred_element_type=_F32)
    o_ref[...] = x + _rms(c, gpost_ref[...])


def _post_mix(ys, wout, h, gmix, gpre, wq, kv, wo, gpost, *, tm, name):
    s, d = h.shape
    nparts = len(ys)
    row = lambda i: (i, 0)
    fixed = lambda i: (0, 0)
    whole = lambda a: pl.BlockSpec(a.shape, fixed, pipeline_mode=pl.Buffered(1))
    in_specs = [pl.BlockSpec((tm, y.shape[1]), row) for y in ys]
    in_specs += [whole(wout), pl.BlockSpec((tm, d), row), whole(gmix), whole(gpre), whole(wq), whole(kv),
                 whole(wo), whole(gpost)]
    bufs = [((tm, y.shape[1]), _BF16, 2) for y in ys]
    bufs += [(wout.shape, _BF16, 1), (wq.shape, _BF16, 1), (wo.shape, _BF16, 1), (kv.shape, _BF16, 1),
             ((tm, d), _F32, 8), ((tm, d), _BF16, 1), ((tm, MEM_W), _F32, 4)]
    return pl.pallas_call(
        functools.partial(_post_mix_kernel, nparts=nparts),
        out_shape=jax.ShapeDtypeStruct((s, d), _F32),
        grid=(s // tm,),
        in_specs=in_specs,
        out_specs=pl.BlockSpec((tm, d), row),
        compiler_params=pltpu.CompilerParams(
            dimension_semantics=("parallel",),
            vmem_limit_bytes=_vmem_limit(bufs)),
        name=name,
    )(*ys, wout, h, gmix, gpre, wq, kv, wo, gpost)


def _mlp_kernel(h_ref, gpre_ref, wup_ref, wdn_ref, gpost_ref, o_ref, hn_ref, acc_ref):
    c = pl.program_id(1)

    @pl.when(c == 0)
    def _():
        hn_ref[...] = _rms(h_ref[...], gpre_ref[...]).astype(_BF16)
        acc_ref[...] = jnp.zeros(acc_ref.shape, _F32)

    u = jnp.maximum(jnp.dot(hn_ref[...], wup_ref[...], preferred_element_type=_F32), 0.0)
    acc_ref[...] += jnp.dot((u * u).astype(_BF16), wdn_ref[...], preferred_element_type=_F32)

    @pl.when(c == pl.num_programs(1) - 1)
    def _():
        o_ref[...] = h_ref[...] + _rms(acc_ref[...], gpost_ref[...])


def _mlp(h, gpre, wup, wdn, gpost, layer, *, tm, tf, name):
    s, d = h.shape
    f = wup.shape[2]
    bufs = [((tm, d), _F32, 5), ((tm, d), _BF16, 1), ((d, tf), _BF16, 4), ((tm, tf), _F32, 2)]
    return pl.pallas_call(
        _mlp_kernel,
        out_shape=jax.ShapeDtypeStruct((s, d), _F32),
        grid=(s // tm, f // tf),
        in_specs=[pl.BlockSpec((tm, d), lambda i, c: (i, 0)),
                  pl.BlockSpec((1, d), lambda i, c: (0, 0)),
                  pl.BlockSpec((None, d, tf), lambda i, c: (layer, 0, c)),
                  pl.BlockSpec((None, tf, d), lambda i, c: (layer, c, 0)),
                  pl.BlockSpec((1, d), lambda i, c: (0, 0))],
        out_specs=pl.BlockSpec((tm, d), lambda i, c: (i, 0)),
        scratch_shapes=[pltpu.VMEM((tm, d), _BF16), pltpu.VMEM((tm, d), _F32)],
        compiler_params=pltpu.CompilerParams(
            dimension_semantics=("parallel", "arbitrary"),
            vmem_limit_bytes=_vmem_limit(bufs)),
        name=name,
    )(h, gpre, wup, wdn, gpost)


NA_QROWS = 4
NA_BAND = NA_QROWS + NA_KH


def _na_kernel(q_ref, k_ref, vt_ref, bfirst_ref, bmid_ref, blast_ref, o_ref, *, blocks_per_step, n_rows):
    step = pl.program_id(1)
    tq = NA_QROWS * GRID_W
    band = NA_BAND * GRID_W
    scs, vts = [], []
    for u in range(blocks_per_step):
        b = step * blocks_per_step + u
        row0 = jnp.clip(b * NA_QROWS - NA_KH // 2, 0, n_rows - NA_BAND)
        k0 = pl.multiple_of(row0 * GRID_W, NA_QROWS * GRID_W)
        sc = lax.dot_general(k_ref[pl.ds(k0, band), :], q_ref[u * tq:(u + 1) * tq, :], _NT,
                             preferred_element_type=_F32)
        bias = bmid_ref[0]
        if u == 0:
            bias = jnp.where(step == 0, bfirst_ref[0], bias)
        if u == blocks_per_step - 1:
            bias = jnp.where(step == n_rows // (NA_QROWS * blocks_per_step) - 1, blast_ref[0], bias)
        scs.append(sc + bias)
        vts.append(vt_ref[:, pl.ds(k0, band)])
    ps, ls = [], []
    for sc in scs:
        m = jnp.max(sc, axis=0, keepdims=True)
        p = jnp.exp2(sc - m)
        ls.append(jnp.sum(p, axis=0, keepdims=True))
        ps.append(p.astype(_BF16))
    for u in range(blocks_per_step):
        ot = jnp.dot(vts[u], ps[u], preferred_element_type=_F32) / ls[u]
        o_ref[u * tq:(u + 1) * tq, :] = ot.T.astype(_BF16)


def _na_bias_table(rpb, n_rows):
    col = np.arange(GRID_W)
    cs = np.clip(col - NA_KW // 2, 0, GRID_W - NA_KW)
    kc = np.arange(GRID_W)
    col_ok = (kc[:, None] >= cs[None, :]) & (kc[:, None] < cs[None, :] + NA_KW)
    coff = kc[:, None] - col[None, :] + (NA_KW - 1)
    n_ro, n_co = 2 * NA_KH - 1, 2 * NA_KW - 1
    onehot = ((np.arange(n_co)[:, None, None] == coff[None]) & col_ok[None]).astype(np.float32)
    cols = jnp.einsum("hrc,ckj->hrkj", rpb * _LOG2E, onehot, precision=lax.Precision.HIGHEST)
    cols = jnp.where(col_ok[None, None], cols, _MASKED)
    cols = jnp.pad(cols, ((0, 0), (NA_BAND, NA_BAND), (0, 0), (0, 0)), constant_values=_MASKED)
    cols = cols.reshape(NA_HEADS, (n_ro + 2 * NA_BAND) * GRID_W, GRID_W)
    tabs = []
    for i0 in (0, 2 * NA_QROWS, n_rows - NA_QROWS):
        row0 = int(np.clip(i0 - NA_KH // 2, 0, n_rows - NA_BAND))
        qi = i0 + np.arange(NA_QROWS)
        rs = np.clip(qi - NA_KH // 2, 0, n_rows - NA_KH)
        krow = row0 + np.arange(NA_BAND)
        row_ok = (krow[:, None] >= rs[None, :]) & (krow[:, None] < rs[None, :] + NA_KH)
        parts = []
        for qr, q in enumerate(qi):
            lo = (NA_BAND + row0 - int(q) + NA_KH - 1) * GRID_W
            ok = np.repeat(row_ok[:, qr], GRID_W)
            parts.append(jnp.where(ok[None, :, None], cols[:, lo:lo + NA_BAND * GRID_W], _MASKED))
        tabs.append(jnp.concatenate(parts, axis=2).astype(_F32))
    return tabs


def _na_attention(proj, vt, bias_tabs, *, blocks_per_step, name):
    s = proj.shape[0]
    n_rows = s // GRID_W
    assert blocks_per_step >= 2
    tq = blocks_per_step * NA_QROWS * GRID_W
    kcol = NA_W // HEAD_DIM
    bshape = (1, NA_BAND * GRID_W, NA_QROWS * GRID_W)
    bspecs = [pl.BlockSpec(bshape, lambda h, i: (h, 0, 0)) for _ in bias_tabs]
    bufs = [((s, HEAD_DIM), _BF16, 4), (bshape, _F32, 6), ((tq, HEAD_DIM), _BF16, 4),
            ((NA_BAND * GRID_W, NA_QROWS * GRID_W), _F32, 3 * blocks_per_step)]
    return pl.pallas_call(
        functools.partial(_na_kernel, blocks_per_step=blocks_per_step, n_rows=n_rows),
        out_shape=jax.ShapeDtypeStruct((s, NA_W), _BF16),
        grid=(NA_HEADS, n_rows // (NA_QROWS * blocks_per_step)),
        in_specs=[pl.BlockSpec((tq, HEAD_DIM), lambda h, i: (i, h)),
                  pl.BlockSpec((s, HEAD_DIM), lambda h, i: (0, kcol + h)),
                  pl.BlockSpec((HEAD_DIM, s), lambda h, i: (h, 0)),
                  *bspecs],
        out_specs=pl.BlockSpec((tq, HEAD_DIM), lambda h, i: (i, h)),
        compiler_params=pltpu.CompilerParams(
            dimension_semantics=("parallel", "arbitrary"),
            vmem_limit_bytes=_vmem_limit(bufs)),
        name=name,
    )(proj, proj, vt, *bias_tabs)


def _swa_kernel(sink_ref, q_ref, k_ref, vt_ref, b_ref, o_ref, *, tq, band, seq):
    n = pl.program_id(0)
    group = SW_Q_HEADS // SW_KV_HEADS
    start = pl.multiple_of(jnp.clip(n * tq - SW_WINDOW, 0, seq - band), SW_WINDOW)
    bias = b_ref[0]
    scs = []
    for kvh in range(SW_KV_HEADS):
        kb = k_ref[pl.ds(start, band), kvh * HEAD_DIM:(kvh + 1) * HEAD_DIM]
        q = jnp.concatenate([q_ref[:, (kvh * group + g) * HEAD_DIM:(kvh * group + g + 1) * HEAD_DIM]
                             for g in range(group)], axis=0)
        scs.append(lax.dot_general(kb, q, _NT, preferred_element_type=_F32))
    ps, denoms = [], []
    for kvh in range(SW_KV_HEADS):
        pg, dg = [], []
        for g in range(group):
            sink = sink_ref[kvh * group + g]
            sc = scs[kvh][:, g * tq:(g + 1) * tq] + bias
            m = jnp.maximum(jnp.max(sc, axis=0, keepdims=True), sink)
            p = jnp.exp2(sc - m)
            dg.append(jnp.sum(p, axis=0, keepdims=True) + jnp.exp2(sink - m))
            pg.append(p.astype(_BF16))
        ps.append(jnp.concatenate(pg, axis=1))
        denoms.append(jnp.concatenate(dg, axis=1))
    for kvh in range(SW_KV_HEADS):
        vt = vt_ref[kvh * HEAD_DIM:(kvh + 1) * HEAD_DIM, pl.ds(start, band)]
        ot = jnp.dot(vt, ps[kvh], preferred_element_type=_F32) / denoms[kvh]
        for g in range(group):
            hd = kvh * group + g
            o_ref[:, hd * HEAD_DIM:(hd + 1) * HEAD_DIM] = ot[:, g * tq:(g + 1) * tq].T.astype(_BF16)


def _swa_bias_table(tq, band):
    kr = np.arange(band)[:, None]
    qc = np.arange(tq)[None, :]
    tabs = [np.where(np.abs(qc - (rel + kr)) <= SW_WINDOW, 0.0, _MASKED)
            for rel in (0, -SW_WINDOW, tq - band)]
    return jnp.asarray(np.stack(tabs), _F32)


def _swa_attention(proj, vt, sinks_log2, *, tq, name):
    s = proj.shape[0]
    band = tq + 2 * SW_WINDOW
    n_blocks = s // tq
    qcol = 3 * NA_W // SW_QW
    kcol = (3 * NA_W + SW_QW) // SW_KVW
    rows = SW_Q_HEADS // SW_KV_HEADS * tq
    bufs = [((s, SW_KVW), _BF16, 4), ((tq, SW_QW), _BF16, 4), ((band, rows), _F32, 8)]
    variant = lambda n: jnp.where(n == 0, 0, jnp.where(n == n_blocks - 1, 2, 1))
    return pl.pallas_call(
        functools.partial(_swa_kernel, tq=tq, band=band, seq=s),
        out_shape=jax.ShapeDtypeStruct((s, SW_QW), _BF16),
        grid=(n_blocks,),
        in_specs=[pl.BlockSpec(memory_space=pltpu.SMEM),
                  pl.BlockSpec((tq, SW_QW), lambda n: (n, qcol)),
                  pl.BlockSpec((s, SW_KVW), lambda n: (0, kcol)),
                  pl.BlockSpec((SW_KVW, s), lambda n: (0, 0)),
                  pl.BlockSpec((1, band, tq), lambda n: (variant(n), 0, 0))],
        out_specs=pl.BlockSpec((tq, SW_QW), lambda n: (n, 0)),
        compiler_params=pltpu.CompilerParams(
            dimension_semantics=("arbitrary",),
            vmem_limit_bytes=_vmem_limit(bufs)),
        name=name,
    )(sinks_log2, proj, proj, vt, _swa_bias_table(tq, band))


def _diff_kernel(q_ref, k_ref, vt_ref, lq1_ref, lk1_ref, lq2_ref, lk2_ref, g_ref, o_ref,
                 m_ref, acc_ref, s_ref, *, tk, seq, lambda_init):
    m_ref[...] = jnp.full(m_ref.shape, -jnp.inf, _F32)
    acc_ref[...] = jnp.zeros(acc_ref.shape, _F32)
    nblk = seq // tk

    def scores(kb, slot):
        k0 = pl.multiple_of(kb * tk, tk)
        for t in range(2):
            sl = slice(t * HEAD_DIM, (t + 1) * HEAD_DIM)
            s_ref[slot + t] = lax.dot_general(k_ref[pl.ds(k0, tk), sl], q_ref[:, sl], _NT,
                                              preferred_element_type=_F32)

    def update(kb, slot):
        k0 = pl.multiple_of(kb * tk, tk)
        vt = jnp.concatenate([vt_ref[:, pl.ds(k0, tk)], jnp.ones((_ONES_ROWS, tk), _BF16)], axis=0)
        for t in range(2):
            sc = s_ref[slot + t]
            m_prev = m_ref[t]
            m_new = jnp.maximum(m_prev, jnp.max(sc, axis=0, keepdims=True))
            alpha = jnp.exp2(m_prev - m_new)
            p = jnp.exp2(sc - m_new)
            acc_ref[t] = alpha * acc_ref[t] + jnp.dot(vt, p.astype(_BF16), preferred_element_type=_F32)
            m_ref[t] = m_new

    scores(0, 0)

    def pair(i, carry):
        scores(2 * i + 1, 2)
        update(2 * i, 0)
        scores(2 * i + 2, 0)
        update(2 * i + 1, 2)
        return carry

    lax.fori_loop(0, nblk // 2 - 1, pair, 0)
    scores(nblk - 1, 2)
    update(nblk - 2, 0)
    update(nblk - 1, 2)

    lam = (jnp.exp(jnp.sum(lq1_ref[...] * lk1_ref[...], axis=-1, keepdims=True))
           - jnp.exp(jnp.sum(lq2_ref[...] * lk2_ref[...], axis=-1, keepdims=True)) + lambda_init)
    hw = 2 * HEAD_DIM
    ot = (acc_ref[0, :hw] / acc_ref[0, hw:hw + 1]
          - lam * (acc_ref[1, :hw] / acc_ref[1, hw:hw + 1]))
    o_ref[...] = (_rms(ot.T, g_ref[...]) * (1.0 - lambda_init)).astype(_BF16)


def _diff_attention(proj, vt, lq1, lk1, lq2, lk2, subln_g, lambda_init, *, tq, tk, name):
    s = proj.shape[0]
    hw = 2 * HEAD_DIM
    kcol = DIFF_W // hw
    vec = pl.BlockSpec((1, HEAD_DIM), lambda h, qb: (0, 0))
    bufs = [((s, hw), _BF16, 4), ((tq, hw), _BF16, 4), ((tk, tq), _F32, 6),
            ((2, hw, tq), _F32, 2)]
    return pl.pallas_call(
        functools.partial(_diff_kernel, tk=tk, seq=s, lambda_init=lambda_init),
        out_shape=jax.ShapeDtypeStruct((s, DIFF_W), _BF16),
        grid=(DIFF_HEADS, s // tq),
        in_specs=[pl.BlockSpec((tq, hw), lambda h, qb: (qb, h)),
                  pl.BlockSpec((s, hw), lambda h, qb: (0, kcol + h)),
                  pl.BlockSpec((hw, s), lambda h, qb: (h, 0)),
                  vec, vec, vec, vec,
                  pl.BlockSpec((1, hw), lambda h, qb: (0, 0))],
        out_specs=pl.BlockSpec((tq, hw), lambda h, qb: (qb, h)),
        scratch_shapes=[pltpu.VMEM((2, 1, tq), _F32),
                        pltpu.VMEM((2, hw + _ONES_ROWS, tq), _F32), pltpu.VMEM((4, tk, tq), _F32)],
        compiler_params=pltpu.CompilerParams(
            dimension_semantics=("parallel", "arbitrary"),
            vmem_limit_bytes=_vmem_limit(bufs)),
        name=name,
    )(proj, proj, vt, lq1, lk1, lq2, lk2, subln_g)


def _rope_tables(seq):
    half = ROPE_DIM // 2
    inv = 1.0 / (ROPE_THETA ** (np.arange(0, ROPE_DIM, 2, dtype=np.float64) / ROPE_DIM))
    ang = np.arange(seq, dtype=np.float64)[:, None] * inv[None, :]
    cos, sin = np.cos(ang), np.sin(ang)
    pad = V7X_LANES - ROPE_DIM
    c = np.concatenate([cos, cos, np.ones((seq, pad))], axis=1)
    s1 = np.concatenate([-sin, np.zeros((seq, pad + half))], axis=1)
    s2 = np.concatenate([np.zeros((seq, half)), sin, np.zeros((seq, pad))], axis=1)
    return tuple(jnp.asarray(t, _F32) for t in (c, s1, s2))


def _col_scale(n, ranges):
    cs = np.ones((1, n), np.float32)
    for lo, hi in ranges:
        cs[:, lo:hi] = _QK_SCALE
    return jnp.asarray(cs)


def kernel(x, mem, even_w_in, even_w_out, na_rpb, sw_sinks, odd_w_in, odd_w_out, diff_lam_q1, diff_lam_k1, diff_lam_q2, diff_lam_k2, diff_subln_g, mix_pre_g, mix_post_g, mem_norm_g, mem_pre_g, mem_post_g, mem_wq, mem_wk, mem_wv, mem_wo, mlp_pre_g, mlp_post_g, mlp_w_up, mlp_w_down):
    b, s, d = x.shape
    assert (b, s, d) == (1, SEQ, D_MODEL)
    h = x.reshape(s, d)
    mem2 = mem.reshape(MEM_LEN, d)
    rope_tabs = _rope_tables(s)
    row = lambda v: v.reshape(1, -1)
    w_up, w_down = mlp_w_up.astype(_BF16), mlp_w_down.astype(_BF16)

    for layer in range(DEPTH):
        if layer % 2 == 0:
            e = layer // 2
            sw_q0 = 3 * NA_W
            proj = _norm_matmul(
                h, row(mix_pre_g[layer]), even_w_in[e],
                _col_scale(EVEN_IN, [(0, NA_W), (sw_q0, sw_q0 + SW_QW)]),
                rope_tabs, (sw_q0, sw_q0 + SW_QW + SW_KVW), tm=1024, tn=512, name=f"in_proj{layer}")
            out_a = _na_attention(proj, proj[:, 2 * NA_W:3 * NA_W].T, _na_bias_table(na_rpb[e], s // GRID_W),
                                  blocks_per_step=16, name=f"na{layer}")
            out_b = _swa_attention(proj, proj[:, EVEN_IN - SW_KVW:].T, sw_sinks[e] * _LOG2E,
                                   tq=256, name=f"swa{layer}")
            ys, w_out = [out_a, out_b], even_w_out[e].astype(_BF16)
        else:
            o = layer // 2
            lambda_init = 0.8 - 0.6 * math.exp(-0.3 * layer)
            proj = _norm_matmul(
                h, row(mix_pre_g[layer]), odd_w_in[o],
                _col_scale(ODD_IN, [(0, DIFF_W)]),
                rope_tabs, (0, 2 * DIFF_W), tm=1024, tn=512, name=f"in_proj{layer}")
            y = _diff_attention(proj, proj[:, 2 * DIFF_W:].T, row(diff_lam_q1[o]), row(diff_lam_k1[o]), row(diff_lam_q2[o]),
                                row(diff_lam_k2[o]), row(diff_subln_g[o]), lambda_init,
                                tq=1024, tk=1024, name=f"diff{layer}")
            ys, w_out = [y], odd_w_out[o].astype(_BF16)

        wkv = jnp.concatenate([mem_wk[layer], mem_wv[layer]], axis=1).astype(_BF16)
        kv = _norm_matmul(mem2, row(mem_norm_g[layer]), wkv, jnp.ones((1, 2 * MEM_W), _F32),
                          tm=MEM_LEN, tn=512, name=f"mem_kv{layer}")
        h = _post_mix(ys, w_out, h, row(mix_post_g[layer]), row(mem_pre_g[layer]),
                      mem_wq[layer].astype(_BF16), kv, mem_wo[layer].astype(_BF16),
                      row(mem_post_g[layer]), tm=512, name=f"post_mix{layer}")
        h = _mlp(h, row(mlp_pre_g[layer]), w_up, w_down, row(mlp_post_g[layer]), layer,
                 tm=512, tf=1024, name=f"mlp{layer}")
    return h.reshape(b, s, d)
```
